```python
import functools
import jax, jax.numpy as jnp
from jax import lax
import numpy as np

D_MODEL = 2048
BATCH = 4
SEQ = 2048
DEPTH = 1
DEC_BATCH = 32
DEC_SEQ = 4
PAST_LEN = 8192
PAGE_SIZE = 128

RWKV_HEAD_DIM = 64
RWKV_WIDTH = D_MODEL // 2
RWKV_HEADS = RWKV_WIDTH // RWKV_HEAD_DIM
DECAY_LORA = 64
ICLR_LORA = 64
GN_EPS = 64e-5
SHIFT_WIDTH = 3 * RWKV_WIDTH + DECAY_LORA + ICLR_LORA

ATT_HEADS = 4
ATT_WIDTH = D_MODEL // 4
ATT_HEAD_DIM = ATT_WIDTH // ATT_HEADS
IDX_HEADS = 16
IDX_DIM = 64
TOPK_MAX = 256
Q_BLOCK = 128
ROPE_THETA = 10000.0

MEM_TOKENS = 256
MEM_HEADS = 4
MEM_WIDTH = D_MODEL // 4
MEM_HEAD_DIM = MEM_WIDTH // MEM_HEADS

MIX_WIDTH = RWKV_WIDTH + ATT_WIDTH + MEM_WIDTH
RMS_EPS = 1e-6
IN_SPLITS = (SHIFT_WIDTH, RWKV_WIDTH,
             ATT_WIDTH, ATT_WIDTH, ATT_WIDTH, ATT_WIDTH,
             IDX_HEADS * IDX_DIM, IDX_HEADS, IDX_DIM,
             MEM_WIDTH, MEM_WIDTH)
IN_WIDTH = sum(IN_SPLITS)

kernel_name = 'hymba_rwkv7_dsa_memory_step'


def rmsnorm(x, w):
    xf = x.astype(jnp.float32)
    y = xf * lax.rsqrt(jnp.mean(xf * xf, axis=-1, keepdims=True) + RMS_EPS)
    return (y * w.astype(jnp.float32)).astype(x.dtype)


def rope(x, pos):
    d = x.shape[-1]
    half = d // 2
    inv_freq = ROPE_THETA ** (-jnp.arange(half, dtype=jnp.float32) * (2.0 / d))
    ang = pos.astype(jnp.float32)[:, None] * inv_freq[None, :]
    cos = jnp.cos(ang)[:, None, :]
    sin = jnp.sin(ang)[:, None, :]
    xf = x.astype(jnp.float32)
    x1, x2 = xf[..., :half], xf[..., half:]
    return jnp.concatenate([x1 * cos - x2 * sin, x2 * cos + x1 * sin], axis=-1).astype(x.dtype)


def rwkv_time_mix(sh, prev_shift, wkv0, mu, w0, w2, a0, a2, k_k, k_a, r_k, gn_w, gn_b):
    Bh, Th, _ = sh.shape
    C = RWKV_WIDTH
    f32 = jnp.float32
    prev = jnp.concatenate([prev_shift[:, None, :].astype(sh.dtype), sh[:, :-1]], axis=1)
    xs = sh + (prev - sh) * mu
    r, k, v, cw, ca = jnp.split(xs, [C, 2 * C, 3 * C, 3 * C + DECAY_LORA], axis=-1)
    w_log = -jax.nn.softplus(-(w0 + jnp.tanh(cw) @ w2).astype(f32)) - 0.5
    decay = jnp.exp(-jnp.exp(w_log))
    a = jax.nn.sigmoid((a0 + ca @ a2).astype(f32))
    hd = lambda t: t.astype(f32).reshape(Bh, Th, RWKV_HEADS, RWKV_HEAD_DIM)
    kk = hd(k * k_k)
    kk = kk / jnp.maximum(jnp.sqrt(jnp.sum(kk * kk, axis=-1, keepdims=True)), 1e-12)
    k = hd(k * (1.0 + (a - 1.0) * k_a))
    r, v, decay, a = hd(r), hd(v), hd(decay), hd(a)

    def step(S, inp):
        r_t, k_t, v_t, w_t, kk_t, a_t = inp
        sa = jnp.einsum('bhvk,bhk->bhv', S, -kk_t)
        S = (S * w_t[:, :, None, :] + sa[..., None] * (kk_t * a_t)[:, :, None, :]
             + v_t[..., None] * k_t[:, :, None, :])
        return S, jnp.einsum('bhvk,bhk->bhv', S, r_t)

    tm = lambda t: jnp.swapaxes(t, 0, 1)
    S_fin, y = lax.scan(step, wkv0.astype(f32), (tm(r), tm(k), tm(v), tm(decay), tm(kk), tm(a)))
    y = tm(y)
    mean = jnp.mean(y, axis=-1, keepdims=True)
    var = jnp.mean(jnp.square(y - mean), axis=-1, keepdims=True)
    y = ((y - mean) * lax.rsqrt(var + GN_EPS) * gn_w.reshape(RWKV_HEADS, RWKV_HEAD_DIM)
         + gn_b.reshape(RWKV_HEADS, RWKV_HEAD_DIM))
    y = y + jnp.sum(r * k * r_k, axis=-1, keepdims=True) * v
    return y.reshape(Bh, Th, C).astype(sh.dtype), S_fin.astype(sh.dtype), sh[:, -1]


def indexer_scores(qi, wi, ki):
    dots = jnp.einsum('bqhd,bld->bqhl', qi, ki).astype(jnp.float32) * IDX_DIM ** -0.5
    return jnp.einsum('bqhl,bqh->bql', jax.nn.relu(dots), wi.astype(jnp.float32) * IDX_HEADS ** -0.5)


def sparse_attend(q, ks, vs, valid):
    s = jnp.einsum('bqhd,bqkhd->bqhk', q, ks).astype(jnp.float32) * ATT_HEAD_DIM ** -0.5
    s = jnp.where(valid[:, :, None, :], s, -jnp.inf)
    p = jax.nn.softmax(s, axis=-1).astype(vs.dtype)
    return jnp.einsum('bqhk,bqkhd->bqhd', p, vs)


_take_rows = jax.vmap(lambda rows, idx: rows[idx])


def dsa_prompt(q, k, v, qi, wi, ki, topk):
    S = q.shape[1]
    key_pos = jnp.arange(S)

    def block(start):
        sl = lambda t: lax.dynamic_slice_in_dim(t, start, Q_BLOCK, axis=1)
        qpos = start + jnp.arange(Q_BLOCK)
        score = indexer_scores(sl(qi), sl(wi), ki)
        score = jnp.where(key_pos[None, None, :] <= qpos[None, :, None], score, -jnp.inf)
        _, idx = lax.top_k(score, topk)
        valid = idx <= qpos[None, :, None]
        return sparse_attend(sl(q), _take_rows(k, idx), _take_rows(v, idx), valid)

    out = lax.map(block, jnp.arange(0, S, Q_BLOCK))
    return jnp.moveaxis(out, 0, 1).reshape(q.shape)


def dsa_sample(q, k, v, qi, wi, ki, cache_k, cache_v, cache_kidx, page_table, topk):
    DB, T = q.shape[:2]
    past = page_table.shape[1] * PAGE_SIZE
    ki_past = cache_kidx[page_table].reshape(DB, past, IDX_DIM).astype(ki.dtype)
    ki_all = jnp.concatenate([ki_past, ki], axis=1)
    qpos = past + jnp.arange(T)
    score = indexer_scores(qi, wi, ki_all)
    score = jnp.where(jnp.arange(past + T)[None, None, :] <= qpos[None, :, None], score, -jnp.inf)
    _, idx = lax.top_k(score, topk)
    valid = idx <= qpos[None, :, None]
    is_past = idx < past
    pidx = jnp.minimum(idx, past - 1)
    phys = _take_rows(page_table, pidx // PAGE_SIZE) * PAGE_SIZE + pidx % PAGE_SIZE
    nidx = jnp.clip(idx - past, 0, T - 1)

    def gather(pool, new):
        from_pool = pool.reshape(-1, ATT_HEADS, ATT_HEAD_DIM)[phys].astype(new.dtype)
        return jnp.where(is_past[..., None, None], from_pool, _take_rows(new, nidx))

    return sparse_attend(q, gather(cache_k, k), gather(cache_v, v), valid)


def memory_kv(mem, mem_norm_w, w_mem_kv):
    Bm, M, _ = mem.shape
    mk, mv = jnp.split(rmsnorm(mem, mem_norm_w) @ w_mem_kv, 2, axis=-1)
    return (mk.reshape(Bm, M, MEM_HEADS, MEM_HEAD_DIM), mv.reshape(Bm, M, MEM_HEADS, MEM_HEAD_DIM))


def mem_attend(q, mk, mv):
    s = jnp.einsum('bthd,bmhd->bhtm', q, mk.astype(q.dtype)).astype(jnp.float32) * MEM_HEAD_DIM ** -0.5
    p = jax.nn.softmax(s, axis=-1).astype(q.dtype)
    return jnp.einsum('bhtm,bmhd->bthd', p, mv.astype(q.dtype))


def mixer_layer(h, pos, prev_shift, wkv0, attend, mem_k, mem_v, norm_w, w_in, w_out, rwkv_params):
    Bh, Th, _ = h.shape
    xn = rmsnorm(h, norm_w)
    offsets = np.cumsum(IN_SPLITS)[:-1].tolist()
    (sh, g_r, q, k, v, g_a, qi, wi, ki, qm, g_m) = jnp.split(xn @ w_in, offsets, axis=-1)
    y_r, wkv, shift = rwkv_time_mix(sh, prev_shift, wkv0, *rwkv_params)
    y_r = y_r * jax.nn.silu(g_r)
    q = rope(q.reshape(Bh, Th, ATT_HEADS, ATT_HEAD_DIM), pos)
    k = rope(k.reshape(Bh, Th, ATT_HEADS, ATT_HEAD_DIM), pos)
    v = v.reshape(Bh, Th, ATT_HEADS, ATT_HEAD_DIM)
    qi = rope(qi.reshape(Bh, Th, IDX_HEADS, IDX_DIM), pos)
    ki = rope(ki[:, :, None, :], pos)[:, :, 0, :]
    y_a = attend(q, k, v, qi, wi, ki).reshape(Bh, Th, ATT_WIDTH) * jax.nn.silu(g_a)
    y_m = mem_attend(qm.reshape(Bh, Th, MEM_HEADS, MEM_HEAD_DIM), mem_k, mem_v).reshape(Bh, Th, MEM_WIDTH)
    y_m = y_m * jax.nn.silu(g_m)
    y = jnp.concatenate([y_r, y_a, y_m], axis=-1) @ w_out
    return h + y.astype(h.dtype), wkv, shift, k, v, ki


def setup_inputs(seed: int = 0) -> dict:
    key = jax.random.key(seed)
    ks = jax.random.split(key, 32)
    f32 = jnp.float32
    n_pages = PAST_LEN // PAGE_SIZE
    n_pool = (DEC_BATCH * n_pages * 5) // 4
    C = RWKV_WIDTH
    nrm = lambda k, shape, s=1.0: s * jax.random.normal(k, shape, f32)
    perm = jax.random.permutation(ks[0], n_pool)
    page_table = perm[:DEC_BATCH * n_pages].reshape(DEC_BATCH, n_pages).astype(jnp.int32)
    return {
        'x_prompt': nrm(ks[1], (BATCH, SEQ, D_MODEL)),
        'x_sample': nrm(ks[2], (DEC_BATCH, DEC_SEQ, D_MODEL)),
        'state_wkv': nrm(ks[3], (DEPTH, DEC_BATCH, RWKV_HEADS, RWKV_HEAD_DIM, RWKV_HEAD_DIM), 0.5),
        'state_shift': nrm(ks[4], (DEPTH, DEC_BATCH, SHIFT_WIDTH)),
        'cache_k': nrm(ks[5], (DEPTH, n_pool, PAGE_SIZE, ATT_HEADS, ATT_HEAD_DIM)),
        'cache_v': nrm(ks[6], (DEPTH, n_pool, PAGE_SIZE, ATT_HEADS, ATT_HEAD_DIM)),
        'cache_kidx': nrm(ks[7], (DEPTH, n_pool, PAGE_SIZE, IDX_DIM)),
        'cache_mem_k': nrm(ks[8], (DEPTH, DEC_BATCH, MEM_TOKENS, MEM_HEADS, MEM_HEAD_DIM)),
        'cache_mem_v': nrm(ks[9], (DEPTH, DEC_BATCH, MEM_TOKENS, MEM_HEADS, MEM_HEAD_DIM)),
        'page_table': page_table,
        'mem_prompt': nrm(ks[10], (BATCH, MEM_TOKENS, D_MODEL)),
        'norm_w': 1.0 + nrm(ks[11], (DEPTH, D_MODEL), 0.02),
        'w_in': nrm(ks[12], (DEPTH, D_MODEL, IN_WIDTH), D_MODEL ** -0.5),
        'shift_mu': jax.random.uniform(ks[13], (DEPTH, SHIFT_WIDTH), f32),
        'w0': jax.random.uniform(ks[14], (DEPTH, C), f32, -6.0, 0.0),
        'w2': nrm(ks[15], (DEPTH, DECAY_LORA, C), 0.1),
        'a0': nrm(ks[16], (DEPTH, C), 0.1),
        'a2': nrm(ks[17], (DEPTH, ICLR_LORA, C), 0.1),
        'k_k': 0.85 + nrm(ks[18], (DEPTH, C), 0.02),
        'k_a': 1.0 + nrm(ks[19], (DEPTH, C), 0.02),
        'r_k': nrm(ks[20], (DEPTH, RWKV_HEADS, RWKV_HEAD_DIM), 0.1),
        'gn_w': 1.0 + nrm(ks[21], (DEPTH, C), 0.02),
        'gn_b': nrm(ks[22], (DEPTH, C), 0.02),
        'mem_norm_w': 1.0 + nrm(ks[23], (DEPTH, D_MODEL), 0.02),
        'w_mem_kv': nrm(ks[24], (DEPTH, D_MODEL, 2 * MEM_WIDTH), D_MODEL ** -0.5),
        'w_out': nrm(ks[25], (DEPTH, MIX_WIDTH, D_MODEL), MIX_WIDTH ** -0.5),
        'final_norm_w': 1.0 + nrm(ks[26], (D_MODEL,), 0.02),
    }


def reference(x_prompt, x_sample, state_wkv, state_shift, cache_k, cache_v, cache_kidx,
              cache_mem_k, cache_mem_v, page_table, mem_prompt, norm_w, w_in, shift_mu,
              w0, w2, a0, a2, k_k, k_a, r_k, gn_w, gn_b, mem_norm_w, w_mem_kv, w_out,
              final_norm_w):
    B, S, _ = x_prompt.shape
    DB, T, _ = x_sample.shape
    past = page_table.shape[1] * PAGE_SIZE
    pos_p = jnp.arange(S)
    pos_s = past + jnp.arange(T)
    topk_p = min(TOPK_MAX, S // 4)
    topk_s = min(TOPK_MAX, (past + T) // 4)
    hp, hs = x_prompt, x_sample
    wkv_p, shift_p, k_p, v_p, kidx_p, memk_p, memv_p = [], [], [], [], [], [], []
    wkv_s, shift_s, k_s, v_s, kidx_s = [], [], [], [], []
    for l in range(DEPTH):
        rw = (shift_mu[l], w0[l], w2[l], a0[l], a2[l], k_k[l], k_a[l], r_k[l], gn_w[l], gn_b[l])
        mk, mv = memory_kv(mem_prompt, mem_norm_w[l], w_mem_kv[l])
        hp, n_wkv, n_shift, n_k, n_v, n_ki = mixer_layer(
            hp, pos_p, jnp.zeros((B, SHIFT_WIDTH), hp.dtype),
            jnp.zeros((B, RWKV_HEADS, RWKV_HEAD_DIM, RWKV_HEAD_DIM), jnp.float32),
            functools.partial(dsa_prompt, topk=topk_p), mk, mv,
            norm_w[l], w_in[l], w_out[l], rw)
        wkv_p.append(n_wkv); shift_p.append(n_shift); k_p.append(n_k); v_p.append(n_v)
        kidx_p.append(n_ki); memk_p.append(mk); memv_p.append(mv)
        attend_s = functools.partial(dsa_sample, cache_k=cache_k[l], cache_v=cache_v[l],
                                     cache_kidx=cache_kidx[l], page_table=page_table, topk=topk_s)
        hs, n_wkv, n_shift, n_k, n_v, n_ki = mixer_layer(
            hs, pos_s, state_shift[l], state_wkv[l], attend_s, cache_mem_k[l], cache_mem_v[l],
            norm_w[l], w_in[l], w_out[l], rw)
        wkv_s.append(n_wkv); shift_s.append(n_shift); k_s.append(n_k); v_s.append(n_v)
        kidx_s.append(n_ki)
    y_prompt = rmsnorm(hp, final_norm_w)
    y_sample = rmsnorm(hs, final_norm_w)
    new_wkv_prompt = jnp.stack(wkv_p)
    new_shift_prompt = jnp.stack(shift_p)
    new_k_prompt = jnp.stack(k_p)
    new_v_prompt = jnp.stack(v_p)
    new_kidx_prompt = jnp.stack(kidx_p)
    new_mem_k_prompt = jnp.stack(memk_p)
    new_mem_v_prompt = jnp.stack(memv_p)
    new_wkv_sample = jnp.stack(wkv_s)
    new_shift_sample = jnp.stack(shift_s)
    new_k_sample = jnp.stack(k_s)
    new_v_sample = jnp.stack(v_s)
    new_kidx_sample = jnp.stack(kidx_s)
    return (y_prompt, y_sample, new_wkv_prompt, new_shift_prompt, new_k_prompt, new_v_prompt,
            new_kidx_prompt, new_mem_k_prompt, new_mem_v_prompt, new_wkv_sample,
            new_shift_sample, new_k_sample, new_v_sample, new_kidx_sample)
```

```python
import functools
import math

import jax
import jax.numpy as jnp
import numpy as np
from jax import lax
from jax.experimental import pallas as pl
from jax.experimental.pallas import tpu as pltpu

F32 = jnp.float32
BF16 = jnp.bfloat16

D_MODEL = 2048
PAGE_SIZE = 128
HEAD_N = 64
RWKV_WIDTH = D_MODEL // 2
RWKV_HEADS = RWKV_WIDTH // HEAD_N
LORA = 64
GN_EPS = 64e-5
SHIFT_WIDTH = 3 * RWKV_WIDTH + 2 * LORA
ATT_HEADS = 4
ATT_WIDTH = D_MODEL // 4
ATT_HEAD_DIM = ATT_WIDTH // ATT_HEADS
IDX_HEADS = 16
IDX_DIM = 64
TOPK_MAX = 256
Q_BLOCK = 128
ROPE_THETA = 10000.0
MEM_HEADS = 4
MEM_WIDTH = D_MODEL // 4
MEM_HEAD_DIM = MEM_WIDTH // MEM_HEADS
RMS_EPS = 1e-6

LANE = 128
SUBLANE = 8
MXU_DIM = 256
VMEM_LIMIT = 56 * 1024 * 1024
NEG = -1e30
SAMPLE_ROWS = 8

COL_SH = 0
COL_GR = SHIFT_WIDTH
COL_KW = COL_GR + RWKV_WIDTH
COL_Q = COL_KW + 3 * LANE
COL_QI = COL_Q + ATT_WIDTH
COL_K = COL_QI + IDX_HEADS * IDX_DIM
COL_V = COL_K + ATT_WIDTH
COL_GA = COL_V + ATT_WIDTH
COL_QM = COL_GA + ATT_WIDTH
COL_GM = COL_QM + MEM_WIDTH
PROJ_WIDTH = COL_GM + MEM_WIDTH
RWKV_COLS = COL_KW


def _cparams(sem):
    return pltpu.CompilerParams(dimension_semantics=sem, vmem_limit_bytes=VMEM_LIMIT)


def _norm_matmul_kernel(x_ref, nw_ref, w_ref, o_ref, xn_ref):
    @pl.when(pl.program_id(1) == 0)
    def _():
        x = x_ref[...]
        ms = jnp.mean(x * x, axis=-1, keepdims=True)
        xn_ref[...] = (x * lax.rsqrt(ms + RMS_EPS) * nw_ref[...]).astype(BF16)

    o_ref[...] = jnp.dot(xn_ref[...], w_ref[...], preferred_element_type=F32)


def _norm_matmul(x, nw, w, tm, tn):
    m, k = x.shape
    n = w.shape[1]
    assert m % tm == 0 and n % tn == 0
    return pl.pallas_call(
        _norm_matmul_kernel,
        grid=(m // tm, n // tn),
        in_specs=[pl.BlockSpec((tm, k), lambda i, j: (i, 0)),
                  pl.BlockSpec((1, k), lambda i, j: (0, 0)),
                  pl.BlockSpec((k, tn), lambda i, j: (0, j))],
        out_specs=pl.BlockSpec((tm, tn), lambda i, j: (i, j)),
        out_shape=jax.ShapeDtypeStruct((m, n), F32),
        scratch_shapes=[pltpu.VMEM((tm, k), BF16)],
        compiler_params=_cparams(("parallel", "arbitrary")),
        name="norm_matmul",
    )(x, nw.reshape(1, k), w)


def _rope_tables(pos):
    def table(d):
        half = d // 2
        inv_freq = ROPE_THETA ** (-jnp.arange(half, dtype=F32) * (2.0 / d))
        ang = pos.astype(F32)[:, None] * inv_freq[None, :]
        cos, sin = jnp.cos(ang), jnp.sin(ang)
        reps = LANE // d
        return (jnp.tile(jnp.concatenate([cos, cos], axis=1), (1, reps)),
                jnp.tile(jnp.concatenate([-sin, sin], axis=1), (1, reps)))
    c128, s128 = table(ATT_HEAD_DIM)
    c64, s64 = table(IDX_DIM)
    return c128, s128, c64, s64


def _rot_half128(x):
    return pltpu.roll(x, LANE // 2, 1)


def _rot_half64(x):
    lane = lax.broadcasted_iota(jnp.int32, x.shape, 1)
    first = (lane & (IDX_DIM - 1)) < IDX_DIM // 2
    return jnp.where(first, pltpu.roll(x, LANE - IDX_DIM // 2, 1), pltpu.roll(x, IDX_DIM // 2, 1))


def _rope_kernel(q_ref, qi_ref, k_ref, v_ref, kw_ref, c128_ref, s128_ref, c64_ref, s64_ref,
                 qr_ref, qihm_ref, kf_ref, kb_ref, vf_ref, vb_ref, kif_ref, kib_ref, *, nb, tper):
    c128, s128 = c128_ref[...], s128_ref[...]
    c64, s64 = c64_ref[...], s64_ref[...]
    for h in range(ATT_HEADS):
        sl = slice(h * LANE, (h + 1) * LANE)
        x = q_ref[:, sl]
        qr_ref[:, sl] = (x * c128 + _rot_half128(x) * s128).astype(qr_ref.dtype)
        x = k_ref[:, sl]
        kr = x * c128 + _rot_half128(x) * s128
        kf_ref[:, sl] = kr
        kb_ref[:, sl] = kr.astype(BF16)
    v = v_ref[...]
    vf_ref[...] = v
    vb_ref[...] = v.astype(BF16)
    for hp in range(IDX_HEADS // 2):
        x = qi_ref[:, hp * LANE:(hp + 1) * LANE]
        xr = x * c64 + _rot_half64(x) * s64
        for sub in range(2):
            piece = xr[:, sub * IDX_DIM:(sub + 1) * IDX_DIM]
            qihm_ref[:, 2 * hp + sub] = piece.reshape(nb, tper, IDX_DIM).astype(qihm_ref.dtype)
    x = kw_ref[...]
    kir = (x * c64 + _rot_half64(x) * s64)[:, :IDX_DIM]
    kif_ref[...] = kir
    kib_ref[...] = kir.astype(BF16)


def _rope(proj, tables, nbatch, tper_total, ts, qi_dtype, q_dtype):
    m = proj.shape[0]
    c128, s128, c64, s64 = tables
    if ts >= tper_total:
        nb, tper = ts // tper_total, tper_total
    else:
        nb, tper = 1, ts
    blocks_per_seq = max(tper_total // ts, 1)
    grid = (m // ts,)

    def col(width, start):
        return pl.BlockSpec((ts, width), lambda i: (i, start // width))

    def tab():
        if nb == 1:
            return pl.BlockSpec((ts, LANE), lambda i: (i % blocks_per_seq, 0))
        return pl.BlockSpec((ts, LANE), lambda i: (0, 0))

    def rows(width):
        return pl.BlockSpec((ts, width), lambda i: (i, 0))

    if nb == 1:
        qihm_spec = pl.BlockSpec((1, IDX_HEADS, tper, IDX_DIM),
                                 lambda i: (i // blocks_per_seq, 0, i % blocks_per_seq, 0))
    else:
        qihm_spec = pl.BlockSpec((nb, IDX_HEADS, tper, IDX_DIM), lambda i: (i, 0, 0, 0))
    if nb > 1:
        c128, s128, c64, s64 = (jnp.tile(t, (nb, 1)) for t in (c128, s128, c64, s64))
    out_shape = [
        jax.ShapeDtypeStruct((m, ATT_WIDTH), q_dtype),
        jax.ShapeDtypeStruct((nbatch, IDX_HEADS, tper_total, IDX_DIM), qi_dtype),
        jax.ShapeDtypeStruct((m, ATT_WIDTH), F32),
        jax.ShapeDtypeStruct((m, ATT_WIDTH), BF16),
        jax.ShapeDtypeStruct((m, ATT_WIDTH), F32),
        jax.ShapeDtypeStruct((m, ATT_WIDTH), BF16),
        jax.ShapeDtypeStruct((m, IDX_DIM), F32),
        jax.ShapeDtypeStruct((m, IDX_DIM), BF16),
    ]
    out_specs = [rows(ATT_WIDTH), qihm_spec, rows(ATT_WIDTH), rows(ATT_WIDTH), rows(ATT_WIDTH),
                 rows(ATT_WIDTH), rows(IDX_DIM), rows(IDX_DIM)]
    return pl.pallas_call(
        functools.partial(_rope_kernel, nb=nb, tper=tper),
        grid=grid,
        in_specs=[col(ATT_WIDTH, COL_Q), col(IDX_HEADS * IDX_DIM, COL_QI), col(ATT_WIDTH, COL_K),
                  col(ATT_WIDTH, COL_V), col(LANE, COL_KW), tab(), tab(), tab(), tab()],
        out_specs=out_specs,
        out_shape=out_shape,
        compiler_params=_cparams(("parallel",)),
        name="rope",
    )(proj, proj, proj, proj, proj, c128, s128, c64, s64)


_HI = lax.Precision.HIGHEST


def _mm(a, b):
    return jnp.dot(a, b, preferred_element_type=F32, precision=_HI)


def _mm_nt(a, b):
    return lax.dot_general(a, b, (((1,), (1,)), ((), ())), preferred_element_type=F32, precision=_HI)


def _mm_tn(a, b):
    return lax.dot_general(a, b, (((0,), (0,)), ((), ())), preferred_element_type=F32, precision=_HI)


def _iota2(shape, dim):
    return lax.broadcasted_iota(jnp.int32, shape, dim)


def _log2(n):
    assert n & (n - 1) == 0
    return n.bit_length() - 1


def _tile_rows(x, n):
    return jnp.concatenate([x] * n, axis=0) if n > 1 else x


def _head_sum(x):
    r = _iota2((MXU_DIM, MXU_DIM), 0) >> _log2(HEAD_N)
    c = _iota2((MXU_DIM, MXU_DIM), 1) >> _log2(HEAD_N)
    ones_bd = jnp.where(r == c, 1.0, 0.0).astype(F32)
    outs = []
    for q in range(RWKV_WIDTH // MXU_DIM):
        outs.append(_mm(x[:, q * MXU_DIM:(q + 1) * MXU_DIM], ones_bd))
    return jnp.concatenate(outs, axis=1)


def _rwkv_kernel(shg_ref, prev_ref, s0_ref, mu_ref, w0_ref, lora_ref, a0_ref, kk_ref, ka_ref,
                 rk_ref, gnw_ref, gnb_ref, y_ref, sout_ref, st_ref, last_ref, *, L, t_valid):
    c = pl.program_id(1)
    C = RWKV_WIDTH

    @pl.when(c == 0)
    def _():
        st_ref[...] = s0_ref[0]
        last_ref[...] = prev_ref[0]

    x = shg_ref[:, :SHIFT_WIDTH]
    g = shg_ref[:, SHIFT_WIDTH:SHIFT_WIDTH + C]
    row = _iota2((L, 1), 0)
    prev = jnp.where(row == 0, last_ref[...], pltpu.roll(x, 1, 0))
    last_ref[...] = x[L - 1:L, :]
    xs = x + (prev - x) * mu_ref[...]
    r = xs[:, 0:C]
    k = xs[:, C:2 * C]
    v = xs[:, 2 * C:3 * C]
    cwa = xs[:, 3 * C:3 * C + 2 * LORA]
    lane = _iota2(cwa.shape, 1)
    cwa = jnp.where(lane < LORA, jnp.tanh(cwa), cwa)
    za = jnp.dot(cwa.astype(BF16), lora_ref[...], preferred_element_type=F32)
    z = w0_ref[...] + za[:, :C]
    a = 1.0 / (1.0 + jnp.exp(-(a0_ref[...] + za[:, C:])))
    ld = (-math.exp(-0.5)) / (1.0 + jnp.exp(-z))
    kk = k * kk_ref[...]
    kmod = k * (1.0 + (a - 1.0) * ka_ref[...])
    sums = _head_sum(jnp.concatenate([kk * kk, r * kmod * rk_ref[...]], axis=0))
    kk = kk / jnp.maximum(jnp.sqrt(sums[:L]), 1e-12)
    bonus = sums[L:] * v
    if t_valid < L:
        valid = row < t_valid
        ld = jnp.where(valid, ld, 0.0)
        kk = jnp.where(valid, kk, 0.0)
        kmod = jnp.where(valid, kmod, 0.0)
        v = jnp.where(valid, v, 0.0)

    tri = jnp.where(_iota2((L, L), 0) >= _iota2((L, L), 1), 1.0, 0.0).astype(F32)
    cum = _mm(tri, ld)
    cum_end = cum[L - 1:L, :]
    p_in = jnp.exp(cum)
    p_ex = jnp.exp(cum - ld)
    p_inv = jnp.exp(-cum)
    p_end = jnp.exp(cum_end - cum)
    p_all = jnp.exp(cum_end)
    kka = kk * a
    al = -kk * p_ex
    rb = r * p_in
    bt = kka * p_inv
    kt = kmod * p_inv
    bend = kka * p_end
    kend = kmod * p_end

    HG = min(RWKV_HEADS, MXU_DIM // L)
    GW = HG * HEAD_N
    SW = HG * L
    lg_l, lg_n = _log2(L), _log2(HEAD_N)
    bd_sc = (_iota2((SW, GW), 0) >> lg_l) == (_iota2((SW, GW), 1) >> lg_n)
    bd_ss = (_iota2((SW, SW), 0) >> lg_l) == (_iota2((SW, SW), 1) >> lg_l)
    t_idx = _iota2((L, SW), 0)
    s_idx = _iota2((L, SW), 1) & (L - 1)
    strict = t_idx > s_idx
    incl = t_idx >= s_idx
    ident = jnp.where(t_idx == s_idx, 1.0, 0.0).astype(F32)
    bd_cc = (_iota2((MXU_DIM, MXU_DIM), 0) >> lg_n) == (_iota2((MXU_DIM, MXU_DIM), 1) >> lg_n)
    hp = MXU_DIM // HEAD_N

    def bd_time(m):
        return jnp.where(bd_ss, _tile_rows(m, HG), 0.0)

    def bd_chan(m):
        return jnp.where(bd_sc, _tile_rows(m, HG), 0.0)

    y_parts = []
    for gi in range(RWKV_HEADS // HG):
        gs = slice(gi * GW, (gi + 1) * GW)
        lhs = jnp.concatenate([al[:, gs], rb[:, gs]], axis=0)
        gb = _mm_nt(lhs, bd_chan(bt[:, gs]))
        gk = _mm_nt(lhs, bd_chan(kt[:, gs]))
        a_ab = jnp.where(strict, gb[:L], 0.0)
        a_rb = jnp.where(incl, gb[L:], 0.0)
        a_ak = jnp.where(strict, gk[:L], 0.0)
        a_rk = jnp.where(incl, gk[L:], 0.0)
        tinv = ident + a_ab
        pw = a_ab
        for _ in range(lg_l - 1):
            pw = _mm(pw, bd_time(pw))
            tinv = tinv + _mm(tinv, bd_time(pw))
        s_parts = []
        for q in range(GW // MXU_DIM):
            qs = slice(gi * GW + q * MXU_DIM, gi * GW + (q + 1) * MXU_DIM)
            sv = st_ref[:, qs]
            sv_bd = jnp.where(bd_cc, _tile_rows(sv, hp), 0.0)
            s_parts.append(_mm_nt(jnp.concatenate([al[:, qs], rb[:, qs]], axis=0), sv_bd))
        s_term = jnp.concatenate(s_parts, axis=1) if len(s_parts) > 1 else s_parts[0]
        v_bd = bd_chan(v[:, gs])
        u = _mm(tinv, bd_chan(s_term[:L] + _mm(a_ak, v_bd)))
        y_parts.append(s_term[L:] + _mm(a_rb, bd_chan(u)) + _mm(a_rk, v_bd))
        for q in range(GW // MXU_DIM):
            qs = slice(gi * GW + q * MXU_DIM, gi * GW + (q + 1) * MXU_DIM)
            ql = slice(q * MXU_DIM, (q + 1) * MXU_DIM)
            vu = jnp.concatenate([v[:, qs], u[:, ql]], axis=0)
            kb = jnp.concatenate([kend[:, qs], bend[:, qs]], axis=0)
            d = _mm_tn(vu, kb)
            col_h = _iota2((HEAD_N, MXU_DIM), 1) >> lg_n
            upd = st_ref[:, qs] * p_all[:, qs]
            for h in range(hp):
                upd = upd + jnp.where(col_h == h, d[h * HEAD_N:(h + 1) * HEAD_N, :], 0.0)
            st_ref[:, qs] = upd
    y = jnp.concatenate(y_parts, axis=1) if len(y_parts) > 1 else y_parts[0]
    mean = _head_sum(y) * (1.0 / HEAD_N)
    yc = y - mean
    var = _head_sum(yc * yc) * (1.0 / HEAD_N)
    yn = yc * lax.rsqrt(var + GN_EPS) * gnw_ref[...] + gnb_ref[...]
    out = (yn + bonus) * (g / (1.0 + jnp.exp(-g)))
    y_ref[...] = out.astype(y_ref.dtype)

    @pl.when(c == pl.num_programs(1) - 1)
    def _():
        sout_ref[0] = st_ref[...]


def _rwkv(proj, prev_shift, s0, params, nbatch, tper, L, t_valid):
    mu, w0, lora, a0, k_k, k_a, r_k, gn_w, gn_b = params
    C = RWKV_WIDTH
    nchunk = tper // L
    vec = lambda n: pl.BlockSpec((1, n), lambda b, c: (0, 0))
    return pl.pallas_call(
        functools.partial(_rwkv_kernel, L=L, t_valid=t_valid),
        grid=(nbatch, nchunk),
        in_specs=[pl.BlockSpec((L, RWKV_COLS), lambda b, c: (b * nchunk + c, 0)),
                  pl.BlockSpec((1, 1, SHIFT_WIDTH), lambda b, c: (b, 0, 0)),
                  pl.BlockSpec((1, HEAD_N, C), lambda b, c: (b, 0, 0)),
                  vec(SHIFT_WIDTH), vec(C),
                  pl.BlockSpec((2 * LORA, 2 * C), lambda b, c: (0, 0)),
                  vec(C), vec(C), vec(C), vec(C), vec(C), vec(C)],
        out_specs=[pl.BlockSpec((L, C), lambda b, c: (b * nchunk + c, 0)),
                   pl.BlockSpec((1, HEAD_N, C), lambda b, c: (b, 0, 0))],
        out_shape=[jax.ShapeDtypeStruct((nbatch * tper, C), BF16),
                   jax.ShapeDtypeStruct((nbatch, HEAD_N, C), F32)],
        scratch_shapes=[pltpu.VMEM((HEAD_N, C), F32), pltpu.VMEM((1, SHIFT_WIDTH), F32)],
        compiler_params=_cparams(("parallel", "arbitrary")),
        name="rwkv",
    )(proj, prev_shift.reshape(nbatch, 1, SHIFT_WIDTH), s0, mu.reshape(1, -1), w0.reshape(1, -1), lora,
      a0.reshape(1, -1), k_k.reshape(1, -1), k_a.reshape(1, -1), r_k.reshape(1, -1),
      gn_w.reshape(1, -1), gn_b.reshape(1, -1))


def _sort_key(score):
    bits = pltpu.bitcast(score, jnp.int32)
    return jnp.where(bits < 0, bits ^ jnp.int32(0x7FFFFFFF), bits)


def _kth_largest_key(keys_ref, nblk, rows, topk):
    int_min = jnp.int32(-2147483648)

    def bit_step(b, ans):
        cand_u = ans | lax.shift_left(jnp.int32(1), jnp.int32(31) - b)
        cand = cand_u ^ int_min

        def blk(j, cnt):
            return cnt + jnp.sum(jnp.where(keys_ref[j] >= cand, 1.0, 0.0), axis=1, keepdims=True)

        cnt = lax.fori_loop(0, nblk, blk, jnp.zeros((rows, 1), F32))
        return jnp.where(cnt >= topk, cand_u, ans)

    ans = lax.fori_loop(0, 32, bit_step, jnp.zeros((rows, 1), jnp.int32))
    return ans ^ int_min


def _select_topk(keys_ref, bias_ref, nblk, rows, topk, causal_fn):
    thr = _kth_largest_key(keys_ref, nblk, rows, topk)

    def cnt_blk(j, cnt):
        return cnt + jnp.sum(jnp.where(keys_ref[j] > thr, 1.0, 0.0), axis=1, keepdims=True)

    n_gt = lax.fori_loop(0, nblk, cnt_blk, jnp.zeros((rows, 1), F32))
    need = topk - n_gt
    tri = jnp.where(_iota2((LANE, LANE), 0) <= _iota2((LANE, LANE), 1), 1.0, 0.0).astype(BF16)

    def sel_blk(j, seen):
        key = keys_ref[j]
        eq = key == thr
        pref = jnp.dot(jnp.where(eq, 1.0, 0.0).astype(BF16), tri, preferred_element_type=F32)
        sel = (key > thr) | (eq & (seen + pref <= need))
        sel = sel & causal_fn(j)
        bias_ref[j] = jnp.where(sel, 0.0, NEG)
        return seen + pref[:, LANE - 1:LANE]

    lax.fori_loop(0, nblk, sel_blk, jnp.zeros((rows, 1), F32))


def _online_softmax_step(s, sel, m, l, acc, v):
    m_new = jnp.maximum(m, jnp.max(jnp.where(sel, s, NEG), axis=1, keepdims=True))
    alpha = jnp.exp(m - m_new)
    p = jnp.where(sel, jnp.exp(s - m_new), 0.0)
    l = alpha * l + jnp.sum(p, axis=1, keepdims=True)
    acc = alpha * acc + jnp.dot(p.astype(BF16), v, preferred_element_type=F32)
    return m_new, l, acc


def _silu(g):
    return g / (1.0 + jnp.exp(-g))


def _dsa_prompt_kernel(qihm_ref, kw_ref, ki_ref, q_ref, k_ref, v_ref, ga_ref, o_ref,
                       keys_ref, bias_ref, *, topk):
    i = pl.program_id(1)
    nblk = i + 1
    QB = Q_BLOCK
    qi2 = qihm_ref[0].reshape(IDX_HEADS * QB, IDX_DIM)
    w = kw_ref[:, IDX_DIM:IDX_DIM + IDX_HEADS] * (IDX_DIM ** -0.5 * IDX_HEADS ** -0.5)
    qpos = i * QB + _iota2((QB, 1), 0)

    def causal(j):
        return (j * QB + _iota2((1, QB), 1)) <= qpos

    def score_blk(j, carry):
        kij = ki_ref[pl.ds(pl.multiple_of(j * QB, QB), QB), :]
        d = lax.dot_general(qi2, kij, (((1,), (1,)), ((), ())), preferred_element_type=F32)
        acc = jnp.zeros((QB, QB), F32)
        for h in range(IDX_HEADS):
            acc = acc + jnp.maximum(d[h * QB:(h + 1) * QB], 0.0) * w[:, h:h + 1]
        acc = jnp.where(causal(j), acc, -jnp.inf)
        keys_ref[j] = _sort_key(acc)
        return carry

    lax.fori_loop(0, nblk, score_blk, 0)
    _select_topk(keys_ref, bias_ref, nblk, QB, topk, causal)

    scale = ATT_HEAD_DIM ** -0.5
    outs = []
    for h in range(ATT_HEADS):
        hs = slice(h * ATT_HEAD_DIM, (h + 1) * ATT_HEAD_DIM)
        qh = q_ref[:, hs]

        def att_blk(j, carry, hs=hs, qh=qh):
            m, l, acc = carry
            rows = pl.ds(pl.multiple_of(j * QB, QB), QB)
            s = lax.dot_general(qh, k_ref[rows, hs], (((1,), (1,)), ((), ())),
                                preferred_element_type=F32) * scale
            return _online_softmax_step(s, bias_ref[j] == 0.0, m, l, acc, v_ref[rows, hs])

        m0 = jnp.full((QB, 1), NEG, F32)
        _, l, acc = lax.fori_loop(0, nblk, att_blk,
                                  (m0, jnp.zeros((QB, 1), F32), jnp.zeros((QB, ATT_HEAD_DIM), F32)))
        outs.append(acc / l)
    o_ref[...] = (jnp.concatenate(outs, axis=1) * _silu(ga_ref[...])).astype(o_ref.dtype)


def _dsa_prompt(proj, qihm, ki_b, q_r, k_b, v_b, nbatch, seq, topk):
    nqb = seq // Q_BLOCK
    rowblk = lambda width, colblk: pl.BlockSpec((Q_BLOCK, width), lambda b, i: (b * nqb + i, colblk))
    seqblk = lambda width: pl.BlockSpec((seq, width), lambda b, i: (b, 0))
    return pl.pallas_call(
        functools.partial(_dsa_prompt_kernel, topk=topk),
        grid=(nbatch, nqb),
        in_specs=[pl.BlockSpec((1, IDX_HEADS, Q_BLOCK, IDX_DIM), lambda b, i: (b, 0, i, 0)),
                  rowblk(LANE, COL_KW // LANE),
                  seqblk(IDX_DIM),
                  rowblk(ATT_WIDTH, 0),
                  seqblk(ATT_WIDTH), seqblk(ATT_WIDTH),
                  rowblk(ATT_WIDTH, COL_GA // ATT_WIDTH)],
        out_specs=rowblk(ATT_WIDTH, 0),
        out_shape=jax.ShapeDtypeStruct((nbatch * seq, ATT_WIDTH), BF16),
        scratch_shapes=[pltpu.VMEM((nqb, Q_BLOCK, Q_BLOCK), jnp.int32),
                        pltpu.VMEM((nqb, Q_BLOCK, Q_BLOCK), F32)],
        compiler_params=_cparams(("parallel", "arbitrary")),
        name="dsa_prompt",
    )(qihm, proj, ki_b, q_r, k_b, v_b, proj)


PAGES_PER_STEP = 16


def _dsa_sample_index_kernel(pt_ref, qihm_ref, kw_ref, kinew_ref, *rest, npages, topk):
    page_refs = rest[:PAGES_PER_STEP]
    bias_ref = rest[PAGES_PER_STEP]
    keys_ref = rest[PAGES_PER_STEP + 1]
    g = pl.program_id(1)
    R = SAMPLE_ROWS
    qi2 = qihm_ref[0].reshape(IDX_HEADS * R, IDX_DIM).astype(BF16)
    w = kw_ref[:, IDX_DIM:IDX_DIM + IDX_HEADS] * (IDX_DIM ** -0.5 * IDX_HEADS ** -0.5)

    def scores(keys_bf16):
        d = lax.dot_general(qi2, keys_bf16, (((1,), (1,)), ((), ())), preferred_element_type=F32)
        acc = jnp.zeros((R, LANE), F32)
        for h in range(IDX_HEADS):
            acc = acc + jnp.maximum(d[h * R:(h + 1) * R], 0.0) * w[:, h:h + 1]
        return acc

    for mth in range(PAGES_PER_STEP):
        keys_ref[g * PAGES_PER_STEP + mth] = _sort_key(scores(page_refs[mth][0].astype(BF16)))

    t_row = _iota2((R, 1), 0)

    def causal(j):
        new_ok = _iota2((1, LANE), 1) <= t_row
        return new_ok | ((jnp.zeros((R, LANE), jnp.int32) + j) < npages)

    @pl.when(g == pl.num_programs(1) - 1)
    def _():
        knew = jnp.concatenate([kinew_ref[...], jnp.zeros((LANE - R, IDX_DIM), F32)], axis=0).astype(BF16)
        sc = jnp.where(_iota2((1, LANE), 1) <= t_row, scores(knew), -jnp.inf)
        keys_ref[npages] = _sort_key(sc)
        _select_topk(keys_ref, bias_ref.at[0], npages + 1, R, topk, causal)


def _dsa_sample_index(proj_s, qihm_s, ki_new, cache_kidx, page_table, topk):
    nbatch, npages = page_table.shape
    assert npages % PAGES_PER_STEP == 0
    ngrp = npages // PAGES_PER_STEP
    R = SAMPLE_ROWS

    def page_spec(mth):
        return pl.BlockSpec((1, PAGE_SIZE, IDX_DIM),
                            lambda b, g, pt, mth=mth: (pt[b, g * PAGES_PER_STEP + mth], 0, 0))

    grid_spec = pltpu.PrefetchScalarGridSpec(
        num_scalar_prefetch=1,
        grid=(nbatch, ngrp),
        in_specs=[pl.BlockSpec((1, IDX_HEADS, R, IDX_DIM), lambda b, g, pt: (b, 0, 0, 0)),
                  pl.BlockSpec((R, LANE), lambda b, g, pt: (b, COL_KW // LANE)),
                  pl.BlockSpec((R, IDX_DIM), lambda b, g, pt: (b, 0))]
                 + [page_spec(mth) for mth in range(PAGES_PER_STEP)],
        out_specs=pl.BlockSpec((1, npages + 1, R, LANE), lambda b, g, pt: (b, 0, 0, 0)),
        scratch_shapes=[pltpu.VMEM((npages + 1, R, LANE), jnp.int32)],
    )
    return pl.pallas_call(
        functools.partial(_dsa_sample_index_kernel, npages=npages, topk=topk),
        grid_spec=grid_spec,
        out_shape=jax.ShapeDtypeStruct((nbatch, npages + 1, R, LANE), F32),
        compiler_params=_cparams(("parallel", "arbitrary")),
        name="dsa_sample_index",
    )(page_table, qihm_s, proj_s, ki_new, *([cache_kidx] * PAGES_PER_STEP))


def _dsa_sample_attn_kernel(pt_ref, bias_ref, biasnew_ref, q_ref, knew_ref, vnew_ref, ga_ref, *rest):
    k_refs = rest[:PAGES_PER_STEP]
    v_refs = rest[PAGES_PER_STEP:2 * PAGES_PER_STEP]
    o_ref, m_ref, l_ref, acc_ref = rest[2 * PAGES_PER_STEP:]
    g = pl.program_id(1)
    R = SAMPLE_ROWS
    scale = ATT_HEAD_DIM ** -0.5

    @pl.when(g == 0)
    def _():
        m_ref[...] = jnp.full(m_ref.shape, NEG, F32)
        l_ref[...] = jnp.zeros(l_ref.shape, F32)
        acc_ref[...] = jnp.zeros(acc_ref.shape, F32)

    q = q_ref[...].astype(BF16)

    def update(h, k_list, v_list, bias_list):
        hs = slice(h * ATT_HEAD_DIM, (h + 1) * ATT_HEAD_DIM)
        qh = q[:, hs]
        s = jnp.concatenate(
            [lax.dot_general(qh, kb, (((1,), (1,)), ((), ())), preferred_element_type=F32) for kb in k_list],
            axis=1) * scale
        sel = (jnp.concatenate(bias_list, axis=1) if len(bias_list) > 1 else bias_list[0]) == 0.0
        m = m_ref[h][:, 0:1]
        l = l_ref[h][:, 0:1]
        m_new = jnp.maximum(m, jnp.max(jnp.where(sel, s, NEG), axis=1, keepdims=True))
        alpha = jnp.exp(m - m_new)
        p = jnp.where(sel, jnp.exp(s - m_new), 0.0)
        l_new = alpha * l + jnp.sum(p, axis=1, keepdims=True)
        pv = jnp.zeros((R, ATT_HEAD_DIM), F32)
        for n, vb in enumerate(v_list):
            pv = pv + jnp.dot(p[:, n * LANE:(n + 1) * LANE].astype(BF16), vb, preferred_element_type=F32)
        acc_ref[h] = alpha * acc_ref[h] + pv
        m_ref[h] = jnp.broadcast_to(m_new, (R, LANE))
        l_ref[h] = jnp.broadcast_to(l_new, (R, LANE))

    sels = [bias_ref[0, mth] for mth in range(PAGES_PER_STEP)]
    for h in range(ATT_HEADS):
        hs = slice(h * ATT_HEAD_DIM, (h + 1) * ATT_HEAD_DIM)
        update(h, [k_refs[mth][0, :, hs].astype(BF16) for mth in range(PAGES_PER_STEP)],
               [v_refs[mth][0, :, hs].astype(BF16) for mth in range(PAGES_PER_STEP)], sels)

    @pl.when(g == pl.num_programs(1) - 1)
    def _():
        pad = jnp.zeros((LANE - R, ATT_WIDTH), F32)
        knew = jnp.concatenate([knew_ref[...], pad], axis=0).astype(BF16)
        vnew = jnp.concatenate([vnew_ref[...], pad], axis=0).astype(BF16)
        sel_new = [biasnew_ref[0, 0]]
        outs = []
        for h in range(ATT_HEADS):
            hs = slice(h * ATT_HEAD_DIM, (h + 1) * ATT_HEAD_DIM)
            update(h, [knew[:, hs]], [vnew[:, hs]], sel_new)
            outs.append(acc_ref[h] / l_ref[h][:, 0:1])
        o_ref[...] = (jnp.concatenate(outs, axis=1) * _silu(ga_ref[...])).astype(o_ref.dtype)


def _dsa_sample_attn(proj_s, bias, q_r, k_new, v_new, cache_k, cache_v, page_table):
    nbatch, npages = page_table.shape
    ngrp = npages // PAGES_PER_STEP
    R = SAMPLE_ROWS

    def page_spec(mth):
        return pl.BlockSpec((1, PAGE_SIZE, ATT_WIDTH),
                            lambda b, g, pt, mth=mth: (pt[b, g * PAGES_PER_STEP + mth], 0, 0))

    rows = lambda width, colblk: pl.BlockSpec((R, width), lambda b, g, pt: (b, colblk))
    grid_spec = pltpu.PrefetchScalarGridSpec(
        num_scalar_prefetch=1,
        grid=(nbatch, ngrp),
        in_specs=[pl.BlockSpec((1, PAGES_PER_STEP, R, LANE), lambda b, g, pt: (b, g, 0, 0)),
                  pl.BlockSpec((1, 1, R, LANE), lambda b, g, pt: (b, npages, 0, 0)),
                  rows(ATT_WIDTH, 0), rows(ATT_WIDTH, 0), rows(ATT_WIDTH, 0),
                  rows(ATT_WIDTH, COL_GA // ATT_WIDTH)]
                 + [page_spec(mth) for mth in range(PAGES_PER_STEP)] * 2,
        out_specs=rows(ATT_WIDTH, 0),
        scratch_shapes=[pltpu.VMEM((ATT_HEADS, R, LANE), F32), pltpu.VMEM((ATT_HEADS, R, LANE), F32),
                        pltpu.VMEM((ATT_HEADS, R, ATT_HEAD_DIM), F32)],
    )
    return pl.pallas_call(
        _dsa_sample_attn_kernel,
        grid_spec=grid_spec,
        out_shape=jax.ShapeDtypeStruct((nbatch * R, ATT_WIDTH), BF16),
        compiler_params=_cparams(("parallel", "arbitrary")),
        name="dsa_sample_attn",
    )(page_table, bias, bias, q_r, k_new, v_new, proj_s,
      *([cache_k] * PAGES_PER_STEP), *([cache_v] * PAGES_PER_STEP))


def _mem_attn_kernel(q_ref, mk_ref, mv_ref, gm_ref, o_ref):
    scale = MEM_HEAD_DIM ** -0.5
    outs = []
    for h in range(MEM_HEADS):
        hs = slice(h * MEM_HEAD_DIM, (h + 1) * MEM_HEAD_DIM)
        s = lax.dot_general(q_ref[:, hs].astype(BF16), mk_ref[:, hs].astype(BF16),
                            (((1,), (1,)), ((), ())), preferred_element_type=F32) * scale
        p = jnp.exp(s - jnp.max(s, axis=1, keepdims=True))
        p = p / jnp.sum(p, axis=1, keepdims=True)
        outs.append(jnp.dot(p.astype(BF16), mv_ref[:, hs].astype(BF16), preferred_element_type=F32))
    o_ref[...] = (jnp.concatenate(outs, axis=1) * _silu(gm_ref[...])).astype(o_ref.dtype)


def _mem_attn(proj, mk, mv, mk_col, mv_col, nbatch, tper, tq):
    nq = tper // tq
    mem_tokens = mk.shape[0] // nbatch
    return pl.pallas_call(
        _mem_attn_kernel,
        grid=(nbatch, nq),
        in_specs=[pl.BlockSpec((tq, MEM_WIDTH), lambda b, i: (b * nq + i, COL_QM // MEM_WIDTH)),
                  pl.BlockSpec((mem_tokens, MEM_WIDTH), lambda b, i: (b, mk_col)),
                  pl.BlockSpec((mem_tokens, MEM_WIDTH), lambda b, i: (b, mv_col)),
                  pl.BlockSpec((tq, MEM_WIDTH), lambda b, i: (b * nq + i, COL_GM // MEM_WIDTH))],
        out_specs=pl.BlockSpec((tq, MEM_WIDTH), lambda b, i: (b * nq + i, 0)),
        out_shape=jax.ShapeDtypeStruct((nbatch * tper, MEM_WIDTH), BF16),
        compiler_params=_cparams(("parallel", "arbitrary")),
        name="mem_attn",
    )(proj, mk, mv, proj)


def _out_proj_kernel(yr_ref, ya_ref, ym_ref, x_ref, w_ref, fnw_ref, o_ref):
    C = RWKV_WIDTH
    y = jnp.dot(yr_ref[...], w_ref[0:C, :], preferred_element_type=F32)
    y = y + jnp.dot(ya_ref[...], w_ref[C:C + ATT_WIDTH, :], preferred_element_type=F32)
    y = y + jnp.dot(ym_ref[...], w_ref[C + ATT_WIDTH:, :], preferred_element_type=F32)
    h = x_ref[...] + y
    ms = jnp.mean(h * h, axis=-1, keepdims=True)
    o_ref[...] = h * lax.rsqrt(ms + RMS_EPS) * fnw_ref[...]


def _out_proj(yr, ya, ym, x, w_out_b, fnw, tm):
    m = x.shape[0]
    rows = lambda width: pl.BlockSpec((tm, width), lambda i: (i, 0))
    return pl.pallas_call(
        _out_proj_kernel,
        grid=(m // tm,),
        in_specs=[rows(RWKV_WIDTH), rows(ATT_WIDTH), rows(MEM_WIDTH), rows(D_MODEL),
                  pl.BlockSpec((D_MODEL, D_MODEL), lambda i: (0, 0)),
                  pl.BlockSpec((1, D_MODEL), lambda i: (0, 0))],
        out_specs=rows(D_MODEL),
        out_shape=jax.ShapeDtypeStruct((m, D_MODEL), F32),
        compiler_params=_cparams(("parallel",)),
        name="out_proj",
    )(yr, ya, ym, x, w_out_b, fnw.reshape(1, D_MODEL))


def _relayout_w_in(w_in):
    C = RWKV_WIDTH
    o = 0
    pieces = {}
    for name, width in (("sh", SHIFT_WIDTH), ("gr", C), ("q", ATT_WIDTH), ("k", ATT_WIDTH),
                        ("v", ATT_WIDTH), ("ga", ATT_WIDTH), ("qi", IDX_HEADS * IDX_DIM),
                        ("wi", IDX_HEADS), ("ki", IDX_DIM), ("qm", MEM_WIDTH), ("gm", MEM_WIDTH)):
        pieces[name] = w_in[:, o:o + width]
        o += width
    d = w_in.shape[0]
    pad = jnp.zeros((d, COL_Q - COL_KW - IDX_DIM - IDX_HEADS), w_in.dtype)
    cols = [pieces["sh"], pieces["gr"], pieces["ki"], pieces["wi"], pad, pieces["q"], pieces["qi"],
            pieces["k"], pieces["v"], pieces["ga"], pieces["qm"], pieces["gm"]]
    out = jnp.concatenate(cols, axis=1).astype(BF16)
    assert out.shape[1] == PROJ_WIDTH
    return out


def _state_to_lanes(s):
    b = s.shape[0]
    return jnp.transpose(s, (0, 2, 1, 3)).reshape(b, HEAD_N, RWKV_WIDTH)


def _state_from_lanes(s):
    b = s.shape[0]
    return jnp.transpose(s.reshape(b, HEAD_N, RWKV_HEADS, HEAD_N), (0, 2, 1, 3))


def _pick_tile(m, target):
    t = min(m, target)
    while m % t:
        t //= 2
    return t


def kernel(x_prompt, x_sample, state_wkv, state_shift, cache_k, cache_v, cache_kidx, cache_mem_k, cache_mem_v, page_table, mem_prompt, norm_w, w_in, shift_mu, w0, w2, a0, a2, k_k, k_a, r_k, gn_w, gn_b, mem_norm_w, w_mem_kv, w_out, final_norm_w):
    B, S, D = x_prompt.shape
    DB, T, _ = x_sample.shape
    R = SAMPLE_ROWS
    assert D == D_MODEL and T <= R and S % Q_BLOCK == 0
    n_pages = page_table.shape[1]
    past = n_pages * PAGE_SIZE
    n_pool = cache_k.shape[1]
    mem_tokens = mem_prompt.shape[1]
    topk_p = min(TOPK_MAX, S // 4)
    topk_s = min(TOPK_MAX, (past + T) // 4)
    C = RWKV_WIDTH
    layer = 0

    w_in_b = _relayout_w_in(w_in[layer])
    w_mem_b = w_mem_kv[layer].astype(BF16)
    w_out_b = w_out[layer].astype(BF16)
    zeros_l = jnp.zeros((LORA, C), F32)
    lora = jnp.concatenate([jnp.concatenate([w2[layer], zeros_l], axis=1),
                            jnp.concatenate([zeros_l, a2[layer]], axis=1)], axis=0).astype(BF16)
    rw = (shift_mu[layer], w0[layer], lora, a0[layer], k_k[layer], k_a[layer], r_k[layer].reshape(C),
          gn_w[layer], gn_b[layer])

    xp = x_prompt.reshape(B * S, D)
    xs = jnp.pad(x_sample, ((0, 0), (0, R - T), (0, 0))).reshape(DB * R, D)
    tn = 512
    proj_p = _norm_matmul(xp, norm_w[layer], w_in_b, _pick_tile(B * S, 1024), tn)
    proj_s = _norm_matmul(xs, norm_w[layer], w_in_b, _pick_tile(DB * R, 256), tn)
    memkv = _norm_matmul(mem_prompt.reshape(B * mem_tokens, D), mem_norm_w[layer], w_mem_b,
                         _pick_tile(B * mem_tokens, 512), tn)

    tab_p = _rope_tables(jnp.arange(S))
    tab_s = _rope_tables(past + jnp.arange(R))
    q_p, qihm_p, kf_p, kb_p, vf_p, vb_p, kif_p, kib_p = _rope(
        proj_p, tab_p, B, S, _pick_tile(S, 256), BF16, BF16)
    q_s, qihm_s, kf_s, _, vf_s, _, kif_s, _ = _rope(
        proj_s, tab_s, DB, R, DB * R, F32, F32)

    L_p = _pick_tile(S, 64)
    yr_p, sfin_p = _rwkv(proj_p, jnp.zeros((B, SHIFT_WIDTH), F32), jnp.zeros((B, HEAD_N, C), F32),
                         rw, B, S, L_p, L_p)
    yr_s, sfin_s = _rwkv(proj_s, state_shift[layer], _state_to_lanes(state_wkv[layer]),
                         rw, DB, R, R, T)

    ya_p = _dsa_prompt(proj_p, qihm_p, kib_p, q_p, kb_p, vb_p, B, S, topk_p)
    bias_s = _dsa_sample_index(proj_s, qihm_s, kif_s, cache_kidx[layer], page_table, topk_s)
    ya_s = _dsa_sample_attn(proj_s, bias_s, q_s, kf_s, vf_s,
                            cache_k[layer].reshape(n_pool, PAGE_SIZE, ATT_WIDTH),
                            cache_v[layer].reshape(n_pool, PAGE_SIZE, ATT_WIDTH), page_table)

    ym_p = _mem_attn(proj_p, memkv, memkv, 0, 1, B, S, _pick_tile(S, 512))
    ym_s = _mem_attn(proj_s, cache_mem_k[layer].reshape(DB * mem_tokens, MEM_WIDTH),
                     cache_mem_v[layer].reshape(DB * mem_tokens, MEM_WIDTH), 0, 0, DB, R, R)

    out_p = _out_proj(yr_p, ya_p, ym_p, xp, w_out_b, final_norm_w, _pick_tile(B * S, 512))
    out_s = _out_proj(yr_s, ya_s, ym_s, xs, w_out_b, final_norm_w, _pick_tile(DB * R, 256))

    def samp(a, *tail):
        return a.reshape(DB, R, *tail)[:, :T]

    y_prompt = out_p.reshape(B, S, D)
    y_sample = samp(out_s, D)
    new_wkv_prompt = _state_from_lanes(sfin_p)[None]
    new_shift_prompt = proj_p.reshape(B, S, PROJ_WIDTH)[:, S - 1, :SHIFT_WIDTH][None]
    new_k_prompt = kf_p.reshape(B, S, ATT_HEADS, ATT_HEAD_DIM)[None]
    new_v_prompt = vf_p.reshape(B, S, ATT_HEADS, ATT_HEAD_DIM)[None]
    new_kidx_prompt = kif_p.reshape(B, S, IDX_DIM)[None]
    new_mem_k_prompt = memkv[:, :MEM_WIDTH].reshape(B, mem_tokens, MEM_HEADS, MEM_HEAD_DIM)[None]
    new_mem_v_prompt = memkv[:, MEM_WIDTH:].reshape(B, mem_tokens, MEM_HEADS, MEM_HEAD_DIM)[None]
    new_wkv_sample = _state_from_lanes(sfin_s)[None]
    new_shift_sample = proj_s.reshape(DB, R, PROJ_WIDTH)[:, T - 1, :SHIFT_WIDTH][None]
    new_k_sample = samp(kf_s, ATT_HEADS, ATT_HEAD_DIM)[None]
    new_v_sample = samp(vf_s, ATT_HEADS, ATT_HEAD_DIM)[None]
    new_kidx_sample = samp(kif_s, IDX_DIM)[None]
    return (y_prompt, y_sample, new_wkv_prompt, new_shift_prompt, new_k_prompt, new_v_prompt,
            new_kidx_prompt, new_mem_k_prompt, new_mem_v_prompt, new_wkv_sample,
            new_shift_sample, new_k_sample, new_v_sample, new_kidx_sample)
```

```python
import functools
import math

import jax
import jax.numpy as jnp
import numpy as np
from jax import lax
from jax.experimental import pallas as pl
from jax.experimental.pallas import tpu as pltpu

F32 = jnp.float32
BF16 = jnp.bfloat16

D_MODEL = 2048
PAGE_SIZE = 128
HEAD_N = 64
RWKV_WIDTH = D_MODEL // 2
RWKV_HEADS = RWKV_WIDTH // HEAD_N
LORA = 64
GN_EPS = 64e-5
SHIFT_WIDTH = 3 * RWKV_WIDTH + 2 * LORA
ATT_HEADS = 4
ATT_WIDTH = D_MODEL // 4
ATT_HEAD_DIM = ATT_WIDTH // ATT_HEADS
IDX_HEADS = 16
IDX_DIM = 64
TOPK_MAX = 256
Q_BLOCK = 128
ROPE_THETA = 10000.0
MEM_HEADS = 4
MEM_WIDTH = D_MODEL // 4
MEM_HEAD_DIM = MEM_WIDTH // MEM_HEADS
RMS_EPS = 1e-6

LANE = 128
SUBLANE = 8
MXU_DIM = 256
VMEM_LIMIT = 56 * 1024 * 1024
NEG = -1e30
SAMPLE_ROWS = 8

COL_SH = 0
COL_GR = SHIFT_WIDTH
COL_KW = COL_GR + RWKV_WIDTH
COL_Q = COL_KW + 3 * LANE
COL_QI = COL_Q + ATT_WIDTH
COL_K = COL_QI + IDX_HEADS * IDX_DIM
COL_V = COL_K + ATT_WIDTH
COL_GA = COL_V + ATT_WIDTH
COL_QM = COL_GA + ATT_WIDTH
COL_GM = COL_QM + MEM_WIDTH
PROJ_WIDTH = COL_GM + MEM_WIDTH
RWKV_COLS = COL_KW


def _cparams(sem):
    return pltpu.CompilerParams(dimension_semantics=sem, vmem_limit_bytes=VMEM_LIMIT)


def _norm_matmul_kernel(x_ref, nw_ref, w_ref, o_ref, xn_ref):
    @pl.when(pl.program_id(1) == 0)
    def _():
        x = x_ref[...]
        ms = jnp.mean(x * x, axis=-1, keepdims=True)
        xn_ref[...] = (x * lax.rsqrt(ms + RMS_EPS) * nw_ref[...]).astype(BF16)

    o_ref[...] = jnp.dot(xn_ref[...], w_ref[...], preferred_element_type=F32)


def _norm_matmul(x, nw, w, tm, tn):
    m, k = x.shape
    n = w.shape[1]
    assert m % tm == 0 and n % tn == 0
    return pl.pallas_call(
        _norm_matmul_kernel,
        grid=(m // tm, n // tn),
        in_specs=[pl.BlockSpec((tm, k), lambda i, j: (i, 0)),
                  pl.BlockSpec((1, k), lambda i, j: (0, 0)),
                  pl.BlockSpec((k, tn), lambda i, j: (0, j))],
        out_specs=pl.BlockSpec((tm, tn), lambda i, j: (i, j)),
        out_shape=jax.ShapeDtypeStruct((m, n), F32),
        scratch_shapes=[pltpu.VMEM((tm, k), BF16)],
        compiler_params=_cparams(("parallel", "arbitrary")),
        name="norm_matmul",
    )(x, nw.reshape(1, k), w)


def _rope_tables(pos):
    def table(d):
        half = d // 2
        inv_freq = ROPE_THETA ** (-jnp.arange(half, dtype=F32) * (2.0 / d))
        ang = pos.astype(F32)[:, None] * inv_freq[None, :]
        cos, sin = jnp.cos(ang), jnp.sin(ang)
        reps = LANE // d
        return (jnp.tile(jnp.concatenate([cos, cos], axis=1), (1, reps)),
                jnp.tile(jnp.concatenate([-sin, sin], axis=1), (1, reps)))
    c128, s128 = table(ATT_HEAD_DIM)
    c64, s64 = table(IDX_DIM)
    return c128, s128, c64, s64


def _rot_half128(x):
    return pltpu.roll(x, LANE // 2, 1)


def _rot_half64(x):
    lane = lax.broadcasted_iota(jnp.int32, x.shape, 1)
    first = (lane & (IDX_DIM - 1)) < IDX_DIM // 2
    return jnp.where(first, pltpu.roll(x, LANE - IDX_DIM // 2, 1), pltpu.roll(x, IDX_DIM // 2, 1))


def _rope_kernel(q_ref, qi_ref, k_ref, v_ref, kw_ref, c128_ref, s128_ref, c64_ref, s64_ref,
                 qr_ref, qihm_ref, kf_ref, kb_ref, vf_ref, vb_ref, kif_ref, kib_ref, *, nb, tper):
    c128, s128 = c128_ref[...], s128_ref[...]
    c64, s64 = c64_ref[...], s64_ref[...]
    for h in range(ATT_HEADS):
        sl = slice(h * LANE, (h + 1) * LANE)
        x = q_ref[:, sl]
        qr_ref[:, sl] = (x * c128 + _rot_half128(x) * s128).astype(qr_ref.dtype)
        x = k_ref[:, sl]
        kr = x * c128 + _rot_half128(x) * s128
        kf_ref[:, sl] = kr
        kb_ref[:, sl] = kr.astype(BF16)
    v = v_ref[...]
    vf_ref[...] = v
    vb_ref[...] = v.astype(BF16)
    for hp in range(IDX_HEADS // 2):
        x = qi_ref[:, hp * LANE:(hp + 1) * LANE]
        xr = x * c64 + _rot_half64(x) * s64
        for sub in range(2):
            piece = xr[:, sub * IDX_DIM:(sub + 1) * IDX_DIM]
            qihm_ref[:, 2 * hp + sub] = piece.reshape(nb, tper, IDX_DIM).astype(qihm_ref.dtype)
    x = kw_ref[...]
    kir = (x * c64 + _rot_half64(x) * s64)[:, :IDX_DIM]
    kif_ref[...] = kir
    kib_ref[...] = kir.astype(BF16)


def _rope(proj, tables, nbatch, tper_total, ts, qi_dtype, q_dtype):
    m = proj.shape[0]
    c128, s128, c64, s64 = tables
    if ts >= tper_total:
        nb, tper = ts // tper_total, tper_total
    else:
        nb, tper = 1, ts
    blocks_per_seq = max(tper_total // ts, 1)
    grid = (m // ts,)

    def col(width, start):
        return pl.BlockSpec((ts, width), lambda i: (i, start // width))

    def tab():
        if nb == 1:
            return pl.BlockSpec((ts, LANE), lambda i: (i % blocks_per_seq, 0))
        return pl.BlockSpec((ts, LANE), lambda i: (0, 0))

    def rows(width):
        return pl.BlockSpec((ts, width), lambda i: (i, 0))

    if nb == 1:
        qihm_spec = pl.BlockSpec((1, IDX_HEADS, tper, IDX_DIM),
                                 lambda i: (i // blocks_per_seq, 0, i % blocks_per_seq, 0))
    else:
        qihm_spec = pl.BlockSpec((nb, IDX_HEADS, tper, IDX_DIM), lambda i: (i, 0, 0, 0))
    if nb > 1:
        c128, s128, c64, s64 = (jnp.tile(t, (nb, 1)) for t in (c128, s128, c64, s64))
    out_shape = [
        jax.ShapeDtypeStruct((m, ATT_WIDTH), q_dtype),
        jax.ShapeDtypeStruct((nbatch, IDX_HEADS, tper_total, IDX_DIM), qi_dtype),
        jax.ShapeDtypeStruct((m, ATT_WIDTH), F32),
        jax.ShapeDtypeStruct((m, ATT_WIDTH), BF16),
        jax.ShapeDtypeStruct((m, ATT_WIDTH), F32),
        jax.ShapeDtypeStruct((m, ATT_WIDTH), BF16),
        jax.ShapeDtypeStruct((m, IDX_DIM), F32),
        jax.ShapeDtypeStruct((m, IDX_DIM), BF16),
    ]
    out_specs = [rows(ATT_WIDTH), qihm_spec, rows(ATT_WIDTH), rows(ATT_WIDTH), rows(ATT_WIDTH),
                 rows(ATT_WIDTH), rows(IDX_DIM), rows(IDX_DIM)]
    return pl.pallas_call(
        functools.partial(_rope_kernel, nb=nb, tper=tper),
        grid=grid,
        in_specs=[col(ATT_WIDTH, COL_Q), col(IDX_HEADS * IDX_DIM, COL_QI), col(ATT_WIDTH, COL_K),
                  col(ATT_WIDTH, COL_V), col(LANE, COL_KW), tab(), tab(), tab(), tab()],
        out_specs=out_specs,
        out_shape=out_shape,
        compiler_params=_cparams(("parallel",)),
        name="rope",
    )(proj, proj, proj, proj, proj, c128, s128, c64, s64)


def _bdot(a, b, dims):
    return lax.dot_general(a.astype(BF16), b.astype(BF16), (dims, ((), ())), preferred_element_type=F32)


def _mm(a, b):
    return _bdot(a, b, ((1,), (0,)))


def _mm_nt(a, b):
    return _bdot(a, b, ((1,), (1,)))


def _mm_tn(a, b):
    return _bdot(a, b, ((0,), (0,)))


def _split_bf16(x, pieces):
    out = []
    for _ in range(pieces - 1):
        hi = x.astype(BF16)
        out.append(hi)
        x = x - hi.astype(F32)
    out.append(x.astype(BF16))
    return out


def _iota2(shape, dim):
    return lax.broadcasted_iota(jnp.int32, shape, dim)


def _log2(n):
    assert n & (n - 1) == 0
    return n.bit_length() - 1


def _tile_rows(x, n):
    return jnp.concatenate([x] * n, axis=0) if n > 1 else x


def _head_sum(x):
    r = _iota2((MXU_DIM, MXU_DIM), 0) >> _log2(HEAD_N)
    c = _iota2((MXU_DIM, MXU_DIM), 1) >> _log2(HEAD_N)
    ones_bd = jnp.where(r == c, 1.0, 0.0).astype(BF16)
    pieces = _split_bf16(x, 2)
    outs = []
    for q in range(RWKV_WIDTH // MXU_DIM):
        qs = slice(q * MXU_DIM, (q + 1) * MXU_DIM)
        outs.append(sum(jnp.dot(p[:, qs], ones_bd, preferred_element_type=F32) for p in pieces))
    return jnp.concatenate(outs, axis=1)


def _rwkv_kernel(shg_ref, prev_ref, s0_ref, mu_ref, w0_ref, lora_ref, a0_ref, kk_ref, ka_ref,
                 rk_ref, gnw_ref, gnb_ref, y_ref, sout_ref, st_ref, last_ref, *, L, t_valid):
    c = pl.program_id(1)
    C = RWKV_WIDTH

    @pl.when(c == 0)
    def _():
        st_ref[...] = s0_ref[0]
        last_ref[...] = prev_ref[0]

    x = shg_ref[:, :SHIFT_WIDTH]
    g = shg_ref[:, SHIFT_WIDTH:SHIFT_WIDTH + C]
    row = _iota2((L, 1), 0)
    prev = jnp.where(row == 0, last_ref[...], pltpu.roll(x, 1, 0))
    last_ref[...] = x[L - 1:L, :]
    xs = x + (prev - x) * mu_ref[...]
    r = xs[:, 0:C]
    k = xs[:, C:2 * C]
    v = xs[:, 2 * C:3 * C]
    cwa = xs[:, 3 * C:3 * C + 2 * LORA]
    lane = _iota2(cwa.shape, 1)
    cwa = jnp.where(lane < LORA, jnp.tanh(cwa), cwa)
    za = jnp.dot(cwa.astype(BF16), lora_ref[...], preferred_element_type=F32)
    z = w0_ref[...] + za[:, :C]
    a = 1.0 / (1.0 + jnp.exp(-(a0_ref[...] + za[:, C:])))
    ld = (-math.exp(-0.5)) / (1.0 + jnp.exp(-z))
    kk = k * kk_ref[...]
    kmod = k * (1.0 + (a - 1.0) * ka_ref[...])
    sums = _head_sum(jnp.concatenate([kk * kk, r * kmod * rk_ref[...]], axis=0))
    kk = kk / jnp.maximum(jnp.sqrt(sums[:L]), 1e-12)
    bonus = sums[L:] * v
    if t_valid < L:
        valid = row < t_valid
        ld = jnp.where(valid, ld, 0.0)
        kk = jnp.where(valid, kk, 0.0)
        kmod = jnp.where(valid, kmod, 0.0)
        v = jnp.where(valid, v, 0.0)

    tri = jnp.where(_iota2((L, L), 0) >= _iota2((L, L), 1), 1.0, 0.0).astype(BF16)
    cum = sum(jnp.dot(tri, p, preferred_element_type=F32) for p in _split_bf16(ld, 3))
    cum_end = cum[L - 1:L, :]
    p_in = jnp.exp(cum)
    p_ex = jnp.exp(cum - ld)
    p_inv = jnp.exp(-cum)
    p_end = jnp.exp(cum_end - cum)
    p_all = jnp.exp(cum_end)
    kka = kk * a
    al = -kk * p_ex
    rb = r * p_in
    bt = kka * p_inv
    kt = kmod * p_inv
    bend = kka * p_end
    kend = kmod * p_end

    HG = min(RWKV_HEADS, MXU_DIM // L)
    GW = HG * HEAD_N
    SW = HG * L
    lg_l, lg_n = _log2(L), _log2(HEAD_N)
    bd_sc = (_iota2((SW, GW), 0) >> lg_l) == (_iota2((SW, GW), 1) >> lg_n)
    bd_ss = (_iota2((SW, SW), 0) >> lg_l) == (_iota2((SW, SW), 1) >> lg_l)
    t_idx = _iota2((L, SW), 0)
    s_idx = _iota2((L, SW), 1) & (L - 1)
    strict = t_idx > s_idx
    incl = t_idx >= s_idx
    ident = jnp.where(t_idx == s_idx, 1.0, 0.0).astype(F32)
    bd_cc = (_iota2((MXU_DIM, MXU_DIM), 0) >> lg_n) == (_iota2((MXU_DIM, MXU_DIM), 1) >> lg_n)
    hp = MXU_DIM // HEAD_N

    def bd_time(m):
        return jnp.where(bd_ss, _tile_rows(m, HG), 0.0)

    def bd_chan(m):
        return jnp.where(bd_sc, _tile_rows(m, HG), 0.0)

    y_parts = []
    for gi in range(RWKV_HEADS // HG):
        gs = slice(gi * GW, (gi + 1) * GW)
        lhs = jnp.concatenate([al[:, gs], rb[:, gs]], axis=0)
        gb = _mm_nt(lhs, bd_chan(bt[:, gs]))
        gk = _mm_nt(lhs, bd_chan(kt[:, gs]))
        a_ab = jnp.where(strict, gb[:L], 0.0)
        a_rb = jnp.where(incl, gb[L:], 0.0)
        a_ak = jnp.where(strict, gk[:L], 0.0)
        a_rk = jnp.where(incl, gk[L:], 0.0)
        tinv = ident + a_ab
        pw = a_ab
        for _ in range(lg_l - 1):
            pw = _mm(pw, bd_time(pw))
            tinv = tinv + _mm(tinv, bd_time(pw))
        s_parts = []
        for q in range(GW // MXU_DIM):
            qs = slice(gi * GW + q * MXU_DIM, gi * GW + (q + 1) * MXU_DIM)
            sv = st_ref[:, qs]
            sv_bd = jnp.where(bd_cc, _tile_rows(sv, hp), 0.0)
            s_parts.append(_mm_nt(jnp.concatenate([al[:, qs], rb[:, qs]], axis=0), sv_bd))
        s_term = jnp.concatenate(s_parts, axis=1) if len(s_parts) > 1 else s_parts[0]
        v_bd = bd_chan(v[:, gs])
        u = _mm(tinv, bd_chan(s_term[:L] + _mm(a_ak, v_bd)))
        y_parts.append(s_term[L:] + _mm(a_rb, bd_chan(u)) + _mm(a_rk, v_bd))
        for q in range(GW // MXU_DIM):
            qs = slice(gi * GW + q * MXU_DIM, gi * GW + (q + 1) * MXU_DIM)
            ql = slice(q * MXU_DIM, (q + 1) * MXU_DIM)
            vu = jnp.concatenate([v[:, qs], u[:, ql]], axis=0)
            kb = jnp.concatenate([kend[:, qs], bend[:, qs]], axis=0)
            d = _mm_tn(vu, kb)
            col_h = _iota2((HEAD_N, MXU_DIM), 1) >> lg_n
            upd = st_ref[:, qs] * p_all[:, qs]
            for h in range(hp):
                upd = upd + jnp.where(col_h == h, d[h * HEAD_N:(h + 1) * HEAD_N, :], 0.0)
            st_ref[:, qs] = upd
    y = jnp.concatenate(y_parts, axis=1) if len(y_parts) > 1 else y_parts[0]
    mean = _head_sum(y) * (1.0 / HEAD_N)
    yc = y - mean
    var = _head_sum(yc * yc) * (1.0 / HEAD_N)
    yn = yc * lax.rsqrt(var + GN_EPS) * gnw_ref[...] + gnb_ref[...]
    out = (yn + bonus) * (g / (1.0 + jnp.exp(-g)))
    y_ref[...] = out.astype(y_ref.dtype)

    @pl.when(c == pl.num_programs(1) - 1)
    def _():
        sout_ref[0] = st_ref[...]


def _rwkv(proj, prev_shift, s0, params, nbatch, tper, L, t_valid):
    mu, w0, lora, a0, k_k, k_a, r_k, gn_w, gn_b = params
    C = RWKV_WIDTH
    nchunk = tper // L
    vec = lambda n: pl.BlockSpec((1, n), lambda b, c: (0, 0))
    return pl.pallas_call(
        functools.partial(_rwkv_kernel, L=L, t_valid=t_valid),
        grid=(nbatch, nchunk),
        in_specs=[pl.BlockSpec((L, RWKV_COLS), lambda b, c: (b * nchunk + c, 0)),
                  pl.BlockSpec((1, 1, SHIFT_WIDTH), lambda b, c: (b, 0, 0)),
                  pl.BlockSpec((1, HEAD_N, C), lambda b, c: (b, 0, 0)),
                  vec(SHIFT_WIDTH), vec(C),
                  pl.BlockSpec((2 * LORA, 2 * C), lambda b, c: (0, 0)),
                  vec(C), vec(C), vec(C), vec(C), vec(C), vec(C)],
        out_specs=[pl.BlockSpec((L, C), lambda b, c: (b * nchunk + c, 0)),
                   pl.BlockSpec((1, HEAD_N, C), lambda b, c: (b, 0, 0))],
        out_shape=[jax.ShapeDtypeStruct((nbatch * tper, C), BF16),
                   jax.ShapeDtypeStruct((nbatch, HEAD_N, C), F32)],
        scratch_shapes=[pltpu.VMEM((HEAD_N, C), F32), pltpu.VMEM((1, SHIFT_WIDTH), F32)],
        compiler_params=_cparams(("parallel", "arbitrary")),
        name="rwkv",
    )(proj, prev_shift.reshape(nbatch, 1, SHIFT_WIDTH), s0, mu.reshape(1, -1), w0.reshape(1, -1), lora,
      a0.reshape(1, -1), k_k.reshape(1, -1), k_a.reshape(1, -1), r_k.reshape(1, -1),
      gn_w.reshape(1, -1), gn_b.reshape(1, -1))


def _sort_key(score):
    bits = pltpu.bitcast(score, jnp.int32)
    return jnp.where(bits < 0, bits ^ jnp.int32(0x7FFFFFFF), bits)


def _count_where(keys_ref, nblk, rows, pred):
    def blk(j, acc):
        return acc + jnp.where(pred(keys_ref[j]), 1.0, 0.0)

    acc = lax.fori_loop(0, nblk, blk, jnp.zeros((rows, LANE), F32), unroll=isinstance(nblk, int))
    return jnp.sum(acc, axis=1, keepdims=True)


def _kth_largest_key(keys_ref, nblk, rows, topk):
    int_min = jnp.int32(-2147483648)

    def bit_step(b, ans):
        cand_u = ans | lax.shift_left(jnp.int32(1), jnp.int32(31) - b)
        cand = cand_u ^ int_min
        cnt = _count_where(keys_ref, nblk, rows, lambda key: key >= cand)
        return jnp.where(cnt >= topk, cand_u, ans)

    ans = lax.fori_loop(0, 32, bit_step, jnp.zeros((rows, 1), jnp.int32))
    return ans ^ int_min


def _select_topk(keys_ref, bias_ref, nblk, rows, topk, causal_fn):
    thr = _kth_largest_key(keys_ref, nblk, rows, topk)
    n_gt = _count_where(keys_ref, nblk, rows, lambda key: key > thr)
    n_ge = _count_where(keys_ref, nblk, rows, lambda key: key >= thr)
    tied = jnp.max(n_ge) > topk
    unroll = isinstance(nblk, int)

    @pl.when(jnp.logical_not(tied))
    def _():
        def sel_blk(j, carry):
            bias_ref[j] = jnp.where((keys_ref[j] >= thr) & causal_fn(j), 0.0, NEG)
            return carry

        lax.fori_loop(0, nblk, sel_blk, 0, unroll=unroll)

    @pl.when(tied)
    def _():
        need = topk - n_gt
        tri = jnp.where(_iota2((LANE, LANE), 0) <= _iota2((LANE, LANE), 1), 1.0, 0.0).astype(BF16)

        def sel_blk(j, seen):
            key = keys_ref[j]
            eq = key == thr
            pref = jnp.dot(jnp.where(eq, 1.0, 0.0).astype(BF16), tri, preferred_element_type=F32)
            sel = (key > thr) | (eq & (seen + pref <= need))
            bias_ref[j] = jnp.where(sel & causal_fn(j), 0.0, NEG)
            return seen + pref[:, LANE - 1:LANE]

        lax.fori_loop(0, nblk, sel_blk, jnp.zeros((rows, 1), F32))


def _online_softmax_step(s, sel, m, l, acc, v):
    m_new = jnp.maximum(m, jnp.max(jnp.where(sel, s, NEG), axis=1, keepdims=True))
    alpha = jnp.exp(m - m_new)
    p = jnp.where(sel, jnp.exp(s - m_new), 0.0)
    l = alpha * l + jnp.sum(p, axis=1, keepdims=True)
    acc = alpha * acc + jnp.dot(p.astype(BF16), v, preferred_element_type=F32)
    return m_new, l, acc


def _silu(g):
    return g / (1.0 + jnp.exp(-g))


def _dsa_prompt_kernel(qihm_ref, kw_ref, ki_ref, q_ref, k_ref, v_ref, ga_ref, o_ref,
                       keys_ref, bias_ref, *, topk):
    i = pl.program_id(1)
    nblk = i + 1
    QB = Q_BLOCK
    qi2 = qihm_ref[0].reshape(IDX_HEADS * QB, IDX_DIM)
    w = kw_ref[:, IDX_DIM:IDX_DIM + IDX_HEADS] * (IDX_DIM ** -0.5 * IDX_HEADS ** -0.5)
    qpos = i * QB + _iota2((QB, 1), 0)

    def causal(j):
        return (j * QB + _iota2((1, QB), 1)) <= qpos

    def score_blk(j, carry):
        kij = ki_ref[pl.ds(pl.multiple_of(j * QB, QB), QB), :]
        d = lax.dot_general(qi2, kij, (((1,), (1,)), ((), ())), preferred_element_type=F32)
        acc = jnp.zeros((QB, QB), F32)
        for h in range(IDX_HEADS):
            acc = acc + jnp.maximum(d[h * QB:(h + 1) * QB], 0.0) * w[:, h:h + 1]
        acc = jnp.where(causal(j), acc, -jnp.inf)
        keys_ref[j] = _sort_key(acc)
        return carry

    lax.fori_loop(0, nblk, score_blk, 0)
    _select_topk(keys_ref, bias_ref, nblk, QB, topk, causal)

    scale = ATT_HEAD_DIM ** -0.5
    outs = []
    for h in range(ATT_HEADS):
        hs = slice(h * ATT_HEAD_DIM, (h + 1) * ATT_HEAD_DIM)
        qh = q_ref[:, hs]

        def att_blk(j, carry, hs=hs, qh=qh):
            m, l, acc = carry
            rows = pl.ds(pl.multiple_of(j * QB, QB), QB)
            s = lax.dot_general(qh, k_ref[rows, hs], (((1,), (1,)), ((), ())),
                                preferred_element_type=F32) * scale
            return _online_softmax_step(s, bias_ref[j] == 0.0, m, l, acc, v_ref[rows, hs])

        m0 = jnp.full((QB, 1), NEG, F32)
        _, l, acc = lax.fori_loop(0, nblk, att_blk,
                                  (m0, jnp.zeros((QB, 1), F32), jnp.zeros((QB, ATT_HEAD_DIM), F32)))
        outs.append(acc / l)
    o_ref[...] = (jnp.concatenate(outs, axis=1) * _silu(ga_ref[...])).astype(o_ref.dtype)


def _dsa_prompt(proj, qihm, ki_b, q_r, k_b, v_b, nbatch, seq, topk):
    nqb = seq // Q_BLOCK
    rowblk = lambda width, colblk: pl.BlockSpec((Q_BLOCK, width), lambda b, i: (b * nqb + i, colblk))
    seqblk = lambda width: pl.BlockSpec((seq, width), lambda b, i: (b, 0))
    return pl.pallas_call(
        functools.partial(_dsa_prompt_kernel, topk=topk),
        grid=(nbatch, nqb),
        in_specs=[pl.BlockSpec((1, IDX_HEADS, Q_BLOCK, IDX_DIM), lambda b, i: (b, 0, i, 0)),
                  rowblk(LANE, COL_KW // LANE),
                  seqblk(IDX_DIM),
                  rowblk(ATT_WIDTH, 0),
                  seqblk(ATT_WIDTH), seqblk(ATT_WIDTH),
                  rowblk(ATT_WIDTH, COL_GA // ATT_WIDTH)],
        out_specs=rowblk(ATT_WIDTH, 0),
        out_shape=jax.ShapeDtypeStruct((nbatch * seq, ATT_WIDTH), BF16),
        scratch_shapes=[pltpu.VMEM((nqb, Q_BLOCK, Q_BLOCK), jnp.int32),
                        pltpu.VMEM((nqb, Q_BLOCK, Q_BLOCK), F32)],
        compiler_params=_cparams(("parallel", "arbitrary")),
        name="dsa_prompt",
    )(qihm, proj, ki_b, q_r, k_b, v_b, proj)


PAGES_PER_STEP = 16


def _dsa_sample_index_kernel(pt_ref, qihm_ref, kw_ref, kinew_ref, *rest, npages, topk):
    page_refs = rest[:PAGES_PER_STEP]
    bias_ref = rest[PAGES_PER_STEP]
    keys_ref = rest[PAGES_PER_STEP + 1]
    g = pl.program_id(1)
    R = SAMPLE_ROWS
    qi2 = qihm_ref[0].reshape(IDX_HEADS * R, IDX_DIM).astype(BF16)
    w = kw_ref[:, IDX_DIM:IDX_DIM + IDX_HEADS] * (IDX_DIM ** -0.5 * IDX_HEADS ** -0.5)

    def scores(keys_bf16):
        d = lax.dot_general(qi2, keys_bf16, (((1,), (1,)), ((), ())), preferred_element_type=F32)
        acc = jnp.zeros((R, LANE), F32)
        for h in range(IDX_HEADS):
            acc = acc + jnp.maximum(d[h * R:(h + 1) * R], 0.0) * w[:, h:h + 1]
        return acc

    for mth in range(PAGES_PER_STEP):
        keys_ref[g * PAGES_PER_STEP + mth] = _sort_key(scores(page_refs[mth][0, 0].astype(BF16)))

    t_row = _iota2((R, 1), 0)

    def causal(j):
        new_ok = _iota2((1, LANE), 1) <= t_row
        return new_ok | ((jnp.zeros((R, LANE), jnp.int32) + j) < npages)

    @pl.when(g == pl.num_programs(1) - 1)
    def _():
        knew = jnp.concatenate([kinew_ref[...], jnp.zeros((LANE - R, IDX_DIM), F32)], axis=0).astype(BF16)
        sc = jnp.where(_iota2((1, LANE), 1) <= t_row, scores(knew), -jnp.inf)
        keys_ref[npages] = _sort_key(sc)
        _select_topk(keys_ref, bias_ref.at[0], npages + 1, R, topk, causal)


def _dsa_sample_index(proj_s, qihm_s, ki_new, cache_kidx, page_table, topk, layer):
    nbatch, npages = page_table.shape
    assert npages % PAGES_PER_STEP == 0
    ngrp = npages // PAGES_PER_STEP
    R = SAMPLE_ROWS

    def page_spec(mth):
        return pl.BlockSpec((1, 1, PAGE_SIZE, IDX_DIM),
                            lambda b, g, pt, mth=mth: (layer, pt[b, g * PAGES_PER_STEP + mth], 0, 0))

    grid_spec = pltpu.PrefetchScalarGridSpec(
        num_scalar_prefetch=1,
        grid=(nbatch, ngrp),
        in_specs=[pl.BlockSpec((1, IDX_HEADS, R, IDX_DIM), lambda b, g, pt: (b, 0, 0, 0)),
                  pl.BlockSpec((R, LANE), lambda b, g, pt: (b, COL_KW // LANE)),
                  pl.BlockSpec((R, IDX_DIM), lambda b, g, pt: (b, 0))]
                 + [page_spec(mth) for mth in range(PAGES_PER_STEP)],
        out_specs=pl.BlockSpec((1, npages + 1, R, LANE), lambda b, g, pt: (b, 0, 0, 0)),
        scratch_shapes=[pltpu.VMEM((npages + 1, R, LANE), jnp.int32)],
    )
    return pl.pallas_call(
        functools.partial(_dsa_sample_index_kernel, npages=npages, topk=topk),
        grid_spec=grid_spec,
        out_shape=jax.ShapeDtypeStruct((nbatch, npages + 1, R, LANE), F32),
        compiler_params=_cparams(("parallel", "arbitrary")),
        name="dsa_sample_index",
    )(page_table, qihm_s, proj_s, ki_new, *([cache_kidx] * PAGES_PER_STEP))


def _dsa_sample_attn_kernel(pt_ref, bias_ref, biasnew_ref, q_ref, knew_ref, vnew_ref, ga_ref, *rest):
    k_refs = rest[:PAGES_PER_STEP]
    v_refs = rest[PAGES_PER_STEP:2 * PAGES_PER_STEP]
    o_ref, m_ref, l_ref, acc_ref = rest[2 * PAGES_PER_STEP:]
    g = pl.program_id(1)
    R = SAMPLE_ROWS
    scale = ATT_HEAD_DIM ** -0.5

    @pl.when(g == 0)
    def _():
        m_ref[...] = jnp.full(m_ref.shape, NEG, F32)
        l_ref[...] = jnp.zeros(l_ref.shape, F32)
        acc_ref[...] = jnp.zeros(acc_ref.shape, F32)

    q = q_ref[...].astype(BF16)

    def update(h, k_list, v_list, bias_list):
        hs = slice(h * ATT_HEAD_DIM, (h + 1) * ATT_HEAD_DIM)
        qh = q[:, hs]
        s = jnp.concatenate(
            [lax.dot_general(qh, kb, (((1,), (1,)), ((), ())), preferred_element_type=F32) for kb in k_list],
            axis=1) * scale
        sel = (jnp.concatenate(bias_list, axis=1) if len(bias_list) > 1 else bias_list[0]) == 0.0
        m = m_ref[h][:, 0:1]
        l = l_ref[h][:, 0:1]
        m_new = jnp.maximum(m, jnp.max(jnp.where(sel, s, NEG), axis=1, keepdims=True))
        alpha = jnp.exp(m - m_new)
        p = jnp.where(sel, jnp.exp(s - m_new), 0.0)
        l_new = alpha * l + jnp.sum(p, axis=1, keepdims=True)
        pv = jnp.zeros((R, ATT_HEAD_DIM), F32)
        for n, vb in enumerate(v_list):
            pv = pv + jnp.dot(p[:, n * LANE:(n + 1) * LANE].astype(BF16), vb, preferred_element_type=F32)
        acc_ref[h] = alpha * acc_ref[h] + pv
        m_ref[h] = jnp.broadcast_to(m_new, (R, LANE))
        l_ref[h] = jnp.broadcast_to(l_new, (R, LANE))

    sels = [bias_ref[0, mth] for mth in range(PAGES_PER_STEP)]
    for h in range(ATT_HEADS):
        hs = slice(h * ATT_HEAD_DIM, (h + 1) * ATT_HEAD_DIM)
        update(h, [k_refs[mth][0, 0, :, h, :].astype(BF16) for mth in range(PAGES_PER_STEP)],
               [v_refs[mth][0, 0, :, h, :].astype(BF16) for mth in range(PAGES_PER_STEP)], sels)

    @pl.when(g == pl.num_programs(1) - 1)
    def _():
        pad = jnp.zeros((LANE - R, ATT_WIDTH), F32)
        knew = jnp.concatenate([knew_ref[...], pad], axis=0).astype(BF16)
        vnew = jnp.concatenate([vnew_ref[...], pad], axis=0).astype(BF16)
        sel_new = [biasnew_ref[0, 0]]
        outs = []
        for h in range(ATT_HEADS):
            hs = slice(h * ATT_HEAD_DIM, (h + 1) * ATT_HEAD_DIM)
            update(h, [knew[:, hs]], [vnew[:, hs]], sel_new)
            outs.append(acc_ref[h] / l_ref[h][:, 0:1])
        o_ref[...] = (jnp.concatenate(outs, axis=1) * _silu(ga_ref[...])).astype(o_ref.dtype)


def _dsa_sample_attn(proj_s, bias, q_r, k_new, v_new, cache_k, cache_v, page_table, layer):
    nbatch, npages = page_table.shape
    ngrp = npages // PAGES_PER_STEP
    R = SAMPLE_ROWS

    def page_spec(mth):
        return pl.BlockSpec((1, 1, PAGE_SIZE, ATT_HEADS, ATT_HEAD_DIM),
                            lambda b, g, pt, mth=mth: (layer, pt[b, g * PAGES_PER_STEP + mth], 0, 0, 0))

    rows = lambda width, colblk: pl.BlockSpec((R, width), lambda b, g, pt: (b, colblk))
    grid_spec = pltpu.PrefetchScalarGridSpec(
        num_scalar_prefetch=1,
        grid=(nbatch, ngrp),
        in_specs=[pl.BlockSpec((1, PAGES_PER_STEP, R, LANE), lambda b, g, pt: (b, g, 0, 0)),
                  pl.BlockSpec((1, 1, R, LANE), lambda b, g, pt: (b, npages, 0, 0)),
                  rows(ATT_WIDTH, 0), rows(ATT_WIDTH, 0), rows(ATT_WIDTH, 0),
                  rows(ATT_WIDTH, COL_GA // ATT_WIDTH)]
                 + [page_spec(mth) for mth in range(PAGES_PER_STEP)] * 2,
        out_specs=rows(ATT_WIDTH, 0),
        scratch_shapes=[pltpu.VMEM((ATT_HEADS, R, LANE), F32), pltpu.VMEM((ATT_HEADS, R, LANE), F32),
                        pltpu.VMEM((ATT_HEADS, R, ATT_HEAD_DIM), F32)],
    )
    return pl.pallas_call(
        _dsa_sample_attn_kernel,
        grid_spec=grid_spec,
        out_shape=jax.ShapeDtypeStruct((nbatch * R, ATT_WIDTH), BF16),
        compiler_params=_cparams(("parallel", "arbitrary")),
        name="dsa_sample_attn",
    )(page_table, bias, bias, q_r, k_new, v_new, proj_s,
      *([cache_k] * PAGES_PER_STEP), *([cache_v] * PAGES_PER_STEP))


def _mem_attn_kernel(q_ref, mk_ref, mv_ref, gm_ref, o_ref, *, cache_layout):
    scale = MEM_HEAD_DIM ** -0.5
    outs = []
    for h in range(MEM_HEADS):
        hs = slice(h * MEM_HEAD_DIM, (h + 1) * MEM_HEAD_DIM)
        if cache_layout:
            mk, mv = mk_ref[0, 0, :, h, :], mv_ref[0, 0, :, h, :]
        else:
            mk, mv = mk_ref[:, hs], mv_ref[:, hs]
        s = lax.dot_general(q_ref[:, hs].astype(BF16), mk.astype(BF16),
                            (((1,), (1,)), ((), ())), preferred_element_type=F32) * scale
        p = jnp.exp(s - jnp.max(s, axis=1, keepdims=True))
        p = p / jnp.sum(p, axis=1, keepdims=True)
        outs.append(jnp.dot(p.astype(BF16), mv.astype(BF16), preferred_element_type=F32))
    o_ref[...] = (jnp.concatenate(outs, axis=1) * _silu(gm_ref[...])).astype(o_ref.dtype)


def _mem_attn(proj, mk, mv, nbatch, tper, tq, layer=None):
    nq = tper // tq
    if layer is None:
        mem_tokens = mk.shape[0] // nbatch
        mem_specs = [pl.BlockSpec((mem_tokens, MEM_WIDTH), lambda b, i: (b, 0)),
                     pl.BlockSpec((mem_tokens, MEM_WIDTH), lambda b, i: (b, 1))]
    else:
        blk = (1, 1) + mk.shape[2:]
        mem_specs = [pl.BlockSpec(blk, lambda b, i: (layer, b, 0, 0, 0))] * 2
    return pl.pallas_call(
        functools.partial(_mem_attn_kernel, cache_layout=layer is not None),
        grid=(nbatch, nq),
        in_specs=[pl.BlockSpec((tq, MEM_WIDTH), lambda b, i: (b * nq + i, COL_QM // MEM_WIDTH))]
                 + mem_specs
                 + [pl.BlockSpec((tq, MEM_WIDTH), lambda b, i: (b * nq + i, COL_GM // MEM_WIDTH))],
        out_specs=pl.BlockSpec((tq, MEM_WIDTH), lambda b, i: (b * nq + i, 0)),
        out_shape=jax.ShapeDtypeStruct((nbatch * tper, MEM_WIDTH), BF16),
        compiler_params=_cparams(("parallel", "arbitrary")),
        name="mem_attn",
    )(proj, mk, mv, proj)


def _out_proj_kernel(yr_ref, ya_ref, ym_ref, x_ref, w_ref, fnw_ref, o_ref):
    C = RWKV_WIDTH
    y = jnp.dot(yr_ref[...], w_ref[0:C, :], preferred_element_type=F32)
    y = y + jnp.dot(ya_ref[...], w_ref[C:C + ATT_WIDTH, :], preferred_element_type=F32)
    y = y + jnp.dot(ym_ref[...], w_ref[C + ATT_WIDTH:, :], preferred_element_type=F32)
    h = x_ref[...] + y
    ms = jnp.mean(h * h, axis=-1, keepdims=True)
    o_ref[...] = h * lax.rsqrt(ms + RMS_EPS) * fnw_ref[...]


def _out_proj(yr, ya, ym, x, w_out_b, fnw, tm):
    m = x.shape[0]
    rows = lambda width: pl.BlockSpec((tm, width), lambda i: (i, 0))
    return pl.pallas_call(
        _out_proj_kernel,
        grid=(m // tm,),
        in_specs=[rows(RWKV_WIDTH), rows(ATT_WIDTH), rows(MEM_WIDTH), rows(D_MODEL),
                  pl.BlockSpec((D_MODEL, D_MODEL), lambda i: (0, 0)),
                  pl.BlockSpec((1, D_MODEL), lambda i: (0, 0))],
        out_specs=rows(D_MODEL),
        out_shape=jax.ShapeDtypeStruct((m, D_MODEL), F32),
        compiler_params=_cparams(("parallel",)),
        name="out_proj",
    )(yr, ya, ym, x, w_out_b, fnw.reshape(1, D_MODEL))


def _relayout_w_in(w_in):
    C = RWKV_WIDTH
    o = 0
    pieces = {}
    for name, width in (("sh", SHIFT_WIDTH), ("gr", C), ("q", ATT_WIDTH), ("k", ATT_WIDTH),
                        ("v", ATT_WIDTH), ("ga", ATT_WIDTH), ("qi", IDX_HEADS * IDX_DIM),
                        ("wi", IDX_HEADS), ("ki", IDX_DIM), ("qm", MEM_WIDTH), ("gm", MEM_WIDTH)):
        pieces[name] = w_in[:, o:o + width]
        o += width
    d = w_in.shape[0]
    pad = jnp.zeros((d, COL_Q - COL_KW - IDX_DIM - IDX_HEADS), w_in.dtype)
    cols = [pieces["sh"], pieces["gr"], pieces["ki"], pieces["wi"], pad, pieces["q"], pieces["qi"],
            pieces["k"], pieces["v"], pieces["ga"], pieces["qm"], pieces["gm"]]
    out = jnp.concatenate(cols, axis=1).astype(BF16)
    assert out.shape[1] == PROJ_WIDTH
    return out


def _state_to_lanes(s):
    b = s.shape[0]
    return jnp.transpose(s, (0, 2, 1, 3)).reshape(b, HEAD_N, RWKV_WIDTH)


def _state_from_lanes(s):
    b = s.shape[0]
    return jnp.transpose(s.reshape(b, HEAD_N, RWKV_HEADS, HEAD_N), (0, 2, 1, 3))


def _pick_tile(m, target):
    t = min(m, target)
    while m % t:
        t //= 2
    return t


def kernel(x_prompt, x_sample, state_wkv, state_shift, cache_k, cache_v, cache_kidx, cache_mem_k, cache_mem_v, page_table, mem_prompt, norm_w, w_in, shift_mu, w0, w2, a0, a2, k_k, k_a, r_k, gn_w, gn_b, mem_norm_w, w_mem_kv, w_out, final_norm_w):
    B, S, D = x_prompt.shape
    DB, T, _ = x_sample.shape
    R = SAMPLE_ROWS
    assert D == D_MODEL and T <= R and S % Q_BLOCK == 0
    n_pages = page_table.shape[1]
    past = n_pages * PAGE_SIZE
    n_pool = cache_k.shape[1]
    mem_tokens = mem_prompt.shape[1]
    topk_p = min(TOPK_MAX, S // 4)
    topk_s = min(TOPK_MAX, (past + T) // 4)
    C = RWKV_WIDTH
    layer = 0

    w_in_b = _relayout_w_in(w_in[layer])
    w_mem_b = w_mem_kv[layer].astype(BF16)
    w_out_b = w_out[layer].astype(BF16)
    zeros_l = jnp.zeros((LORA, C), F32)
    lora = jnp.concatenate([jnp.concatenate([w2[layer], zeros_l], axis=1),
                            jnp.concatenate([zeros_l, a2[layer]], axis=1)], axis=0).astype(BF16)
    rw = (shift_mu[layer], w0[layer], lora, a0[layer], k_k[layer], k_a[layer], r_k[layer].reshape(C),
          gn_w[layer], gn_b[layer])

    xp = x_prompt.reshape(B * S, D)
    xs = jnp.pad(x_sample, ((0, 0), (0, R - T), (0, 0))).reshape(DB * R, D)
    tn = 512
    proj_p = _norm_matmul(xp, norm_w[layer], w_in_b, _pick_tile(B * S, 1024), tn)
    proj_s = _norm_matmul(xs, norm_w[layer], w_in_b, _pick_tile(DB * R, 256), tn)
    memkv = _norm_matmul(mem_prompt.reshape(B * mem_tokens, D), mem_norm_w[layer], w_mem_b,
                         _pick_tile(B * mem_tokens, 512), tn)

    tab_p = _rope_tables(jnp.arange(S))
    tab_s = _rope_tables(past + jnp.arange(R))
    q_p, qihm_p, kf_p, kb_p, vf_p, vb_p, kif_p, kib_p = _rope(
        proj_p, tab_p, B, S, _pick_tile(S, 256), BF16, BF16)
    q_s, qihm_s, kf_s, _, vf_s, _, kif_s, _ = _rope(
        proj_s, tab_s, DB, R, DB * R, F32, F32)

    L_p = _pick_tile(S, 64)
    yr_p, sfin_p = _rwkv(proj_p, jnp.zeros((B, SHIFT_WIDTH), F32), jnp.zeros((B, HEAD_N, C), F32),
                         rw, B, S, L_p, L_p)
    yr_s, sfin_s = _rwkv(proj_s, state_shift[layer], _state_to_lanes(state_wkv[layer]),
                         rw, DB, R, R, T)

    ya_p = _dsa_prompt(proj_p, qihm_p, kib_p, q_p, kb_p, vb_p, B, S, topk_p)
    bias_s = _dsa_sample_index(proj_s, qihm_s, kif_s, cache_kidx, page_table, topk_s, layer)
    ya_s = _dsa_sample_attn(proj_s, bias_s, q_s, kf_s, vf_s, cache_k, cache_v, page_table, layer)

    ym_p = _mem_attn(proj_p, memkv, memkv, B, S, _pick_tile(S, 512))
    ym_s = _mem_attn(proj_s, cache_mem_k, cache_mem_v, DB, R, R, layer)

    out_p = _out_proj(yr_p, ya_p, ym_p, xp, w_out_b, final_norm_w, _pick_tile(B * S, 512))
    out_s = _out_proj(yr_s, ya_s, ym_s, xs, w_out_b, final_norm_w, _pick_tile(DB * R, 256))

    def samp(a, *tail):
        return a.reshape(DB, R, *tail)[:, :T]

    y_prompt = out_p.reshape(B, S, D)
    y_sample = samp(out_s, D)
    new_wkv_prompt = _state_from_lanes(sfin_p)[None]
    new_shift_prompt = proj_p.reshape(B, S, PROJ_WIDTH)[:, S - 1, :SHIFT_WIDTH][None]
    new_k_prompt = kf_p.reshape(B, S, ATT_HEADS, ATT_HEAD_DIM)[None]
    new_v_prompt = vf_p.reshape(B, S, ATT_HEADS, ATT_HEAD_DIM)[None]
    new_kidx_prompt = kif_p.reshape(B, S, IDX_DIM)[None]
    new_mem_k_prompt = memkv[:, :MEM_WIDTH].reshape(B, mem_tokens, MEM_HEADS, MEM_HEAD_DIM)[None]
    new_mem_v_prompt = memkv[:, MEM_WIDTH:].reshape(B, mem_tokens, MEM_HEADS, MEM_HEAD_DIM)[None]
    new_wkv_sample = _state_from_lanes(sfin_s)[None]
    new_shift_sample = proj_s.reshape(DB, R, PROJ_WIDTH)[:, T - 1, :SHIFT_WIDTH][None]
    new_k_sample = samp(kf_s, ATT_HEADS, ATT_HEAD_DIM)[None]
    new_v_sample = samp(vf_s, ATT_HEADS, ATT_HEAD_DIM)[None]
    new_kidx_sample = samp(kif_s, IDX_DIM)[None]
    return (y_prompt, y_sample, new_wkv_prompt, new_shift_prompt, new_k_prompt, new_v_prompt,
            new_kidx_prompt, new_mem_k_prompt, new_mem_v_prompt, new_wkv_sample,
            new_shift_sample, new_k_sample, new_v_sample, new_kidx_sample)
```

```python
import functools
import math

import jax
import jax.numpy as jnp
from jax import lax
from jax.experimental import pallas as pl
from jax.experimental.pallas import tpu as pltpu

F32 = jnp.float32
BF16 = jnp.bfloat16

D_MODEL = 2048
PAGE_SIZE = 128
HEAD_N = 64
RWKV_WIDTH = D_MODEL // 2
RWKV_HEADS = RWKV_WIDTH // HEAD_N
LORA = 64
GN_EPS = 64e-5
SHIFT_WIDTH = 3 * RWKV_WIDTH + 2 * LORA
ATT_HEADS = 4
ATT_WIDTH = D_MODEL // 4
ATT_HEAD_DIM = ATT_WIDTH // ATT_HEADS
IDX_HEADS = 16
IDX_DIM = 64
TOPK_MAX = 256
Q_BLOCK = 128
ROPE_THETA = 10000.0
MEM_HEADS = 4
MEM_WIDTH = D_MODEL // 4
MEM_HEAD_DIM = MEM_WIDTH // MEM_HEADS
RMS_EPS = 1e-6

LANE = 128
SUBLANE = 8
MXU_DIM = 256
VMEM_LIMIT = 56 * 1024 * 1024
NEG = -1e30
SAMPLE_ROWS = SUBLANE

COL_SH = 0
COL_GR = SHIFT_WIDTH
COL_KW = COL_GR + RWKV_WIDTH
COL_Q = COL_KW + 3 * LANE
COL_QI = COL_Q + ATT_WIDTH
COL_K = COL_QI + IDX_HEADS * IDX_DIM
COL_V = COL_K + ATT_WIDTH
COL_GA = COL_V + ATT_WIDTH
COL_QM = COL_GA + ATT_WIDTH
COL_GM = COL_QM + MEM_WIDTH
PROJ_WIDTH = COL_GM + MEM_WIDTH
RWKV_COLS = COL_KW


def _cparams(sem):
    return pltpu.CompilerParams(dimension_semantics=sem, vmem_limit_bytes=VMEM_LIMIT)


def _norm_matmul_kernel(x_ref, nw_ref, w_ref, o_ref, xn_ref):
    @pl.when(pl.program_id(1) == 0)
    def _():
        x = x_ref[...]
        ms = jnp.mean(x * x, axis=-1, keepdims=True)
        xn_ref[...] = (x * lax.rsqrt(ms + RMS_EPS) * nw_ref[...]).astype(BF16)

    o_ref[...] = jnp.dot(xn_ref[...], w_ref[...], preferred_element_type=F32)


def _norm_matmul(x, nw, w, tm, tn):
    m, k = x.shape
    n = w.shape[1]
    assert m % tm == 0 and n % tn == 0
    return pl.pallas_call(
        _norm_matmul_kernel,
        grid=(m // tm, n // tn),
        in_specs=[pl.BlockSpec((tm, k), lambda i, j: (i, 0)),
                  pl.BlockSpec((1, k), lambda i, j: (0, 0)),
                  pl.BlockSpec((k, tn), lambda i, j: (0, j))],
        out_specs=pl.BlockSpec((tm, tn), lambda i, j: (i, j)),
        out_shape=jax.ShapeDtypeStruct((m, n), F32),
        scratch_shapes=[pltpu.VMEM((tm, k), BF16)],
        compiler_params=_cparams(("parallel", "arbitrary")),
        name="norm_matmul",
    )(x, nw.reshape(1, k), w)


def _rope_tables(pos):
    def table(d):
        half = d // 2
        inv_freq = ROPE_THETA ** (-jnp.arange(half, dtype=F32) * (2.0 / d))
        ang = pos.astype(F32)[:, None] * inv_freq[None, :]
        cos, sin = jnp.cos(ang), jnp.sin(ang)
        reps = LANE // d
        return (jnp.tile(jnp.concatenate([cos, cos], axis=1), (1, reps)),
                jnp.tile(jnp.concatenate([-sin, sin], axis=1), (1, reps)))
    c128, s128 = table(ATT_HEAD_DIM)
    c64, s64 = table(IDX_DIM)
    return c128, s128, c64, s64


def _rot_half128(x):
    return pltpu.roll(x, LANE // 2, 1)


def _rot_half64(x):
    lane = lax.broadcasted_iota(jnp.int32, x.shape, 1)
    first = (lane & (IDX_DIM - 1)) < IDX_DIM // 2
    return jnp.where(first, pltpu.roll(x, LANE - IDX_DIM // 2, 1), pltpu.roll(x, IDX_DIM // 2, 1))


def _rope_kernel(q_ref, qi_ref, k_ref, v_ref, kw_ref, c128_ref, s128_ref, c64_ref, s64_ref,
                 qr_ref, qihm_ref, kf_ref, kb_ref, vf_ref, vb_ref, kif_ref, kib_ref, *, nb, tper):
    c128, s128 = c128_ref[...], s128_ref[...]
    c64, s64 = c64_ref[...], s64_ref[...]
    for h in range(ATT_HEADS):
        sl = slice(h * LANE, (h + 1) * LANE)
        x = q_ref[:, sl]
        qr_ref[:, sl] = (x * c128 + _rot_half128(x) * s128).astype(qr_ref.dtype)
        x = k_ref[:, sl]
        kr = x * c128 + _rot_half128(x) * s128
        kb_ref[:, sl] = kr.astype(BF16)
        if nb == 1:
            kf_ref[0, :, h, :] = kr
            vf_ref[0, :, h, :] = v_ref[:, sl]
        else:
            kf_ref[:, sl] = kr
    v = v_ref[...]
    if nb == 1:
        vb_ref[0] = jnp.transpose(v).astype(BF16)
    else:
        vf_ref[...] = v
        vb_ref[...] = v.astype(BF16)
    for hp in range(IDX_HEADS // 2):
        x = qi_ref[:, hp * LANE:(hp + 1) * LANE]
        xr = x * c64 + _rot_half64(x) * s64
        if nb == 1:
            xt = jnp.transpose(xr).astype(qihm_ref.dtype)
            for sub in range(2):
                qihm_ref[0, 2 * hp + sub] = xt[sub * IDX_DIM:(sub + 1) * IDX_DIM]
        else:
            for sub in range(2):
                piece = xr[:, sub * IDX_DIM:(sub + 1) * IDX_DIM]
                qihm_ref[:, 2 * hp + sub] = piece.reshape(nb, tper, IDX_DIM).astype(qihm_ref.dtype)
    x = kw_ref[...]
    kir_slab = x * c64 + _rot_half64(x) * s64
    kir = kir_slab[:, :IDX_DIM]
    if nb == 1:
        kif_ref[0] = jnp.transpose(kir_slab)[:IDX_DIM]
    else:
        kif_ref[...] = kir
    kib_ref[...] = kir.astype(BF16)


def _rope(proj, tables, nbatch, tper_total, ts, qi_dtype, q_dtype):
    m = proj.shape[0]
    c128, s128, c64, s64 = tables
    if ts >= tper_total:
        nb, tper = ts // tper_total, tper_total
    else:
        nb, tper = 1, ts
    blocks_per_seq = max(tper_total // ts, 1)
    grid = (m // ts,)

    def col(width, start):
        return pl.BlockSpec((ts, width), lambda i: (i, start // width))

    def tab():
        if nb == 1:
            return pl.BlockSpec((ts, LANE), lambda i: (i % blocks_per_seq, 0))
        return pl.BlockSpec((ts, LANE), lambda i: (0, 0))

    def rows(width):
        return pl.BlockSpec((ts, width), lambda i: (i, 0))

    if nb == 1:
        qihm_shape = jax.ShapeDtypeStruct((nbatch, IDX_HEADS, IDX_DIM, tper_total), qi_dtype)
        qihm_spec = pl.BlockSpec((1, IDX_HEADS, IDX_DIM, tper),
                                 lambda i: (i // blocks_per_seq, 0, 0, i % blocks_per_seq))
        vb_shape = jax.ShapeDtypeStruct((nbatch, ATT_WIDTH, tper_total), BF16)
        vb_spec = pl.BlockSpec((1, ATT_WIDTH, ts), lambda i: (i // blocks_per_seq, 0, i % blocks_per_seq))
        kvf_shape = jax.ShapeDtypeStruct((nbatch, tper_total, ATT_HEADS, ATT_HEAD_DIM), F32)
        kvf_spec = pl.BlockSpec((1, ts, ATT_HEADS, ATT_HEAD_DIM),
                                lambda i: (i // blocks_per_seq, i % blocks_per_seq, 0, 0))
        kif_shape = jax.ShapeDtypeStruct((nbatch, IDX_DIM, tper_total), F32)
        kif_spec = pl.BlockSpec((1, IDX_DIM, ts), lambda i: (i // blocks_per_seq, 0, i % blocks_per_seq))
    else:
        kvf_shape, kvf_spec = jax.ShapeDtypeStruct((m, ATT_WIDTH), F32), rows(ATT_WIDTH)
        kif_shape, kif_spec = jax.ShapeDtypeStruct((m, IDX_DIM), F32), rows(IDX_DIM)
        qihm_shape = jax.ShapeDtypeStruct((nbatch, IDX_HEADS, tper_total, IDX_DIM), qi_dtype)
        qihm_spec = pl.BlockSpec((nb, IDX_HEADS, tper, IDX_DIM), lambda i: (i, 0, 0, 0))
        vb_shape = jax.ShapeDtypeStruct((m, ATT_WIDTH), BF16)
        vb_spec = rows(ATT_WIDTH)
    if nb > 1:
        c128, s128, c64, s64 = (jnp.tile(t, (nb, 1)) for t in (c128, s128, c64, s64))
    out_shape = [
        jax.ShapeDtypeStruct((m, ATT_WIDTH), q_dtype),
        qihm_shape,
        kvf_shape,
        jax.ShapeDtypeStruct((m, ATT_WIDTH), BF16),
        kvf_shape,
        vb_shape,
        kif_shape,
        jax.ShapeDtypeStruct((m, IDX_DIM), BF16),
    ]
    out_specs = [rows(ATT_WIDTH), qihm_spec, kvf_spec, rows(ATT_WIDTH), kvf_spec,
                 vb_spec, kif_spec, rows(IDX_DIM)]
    return pl.pallas_call(
        functools.partial(_rope_kernel, nb=nb, tper=tper),
        grid=grid,
        in_specs=[col(ATT_WIDTH, COL_Q), col(IDX_HEADS * IDX_DIM, COL_QI), col(ATT_WIDTH, COL_K),
                  col(ATT_WIDTH, COL_V), col(LANE, COL_KW), tab(), tab(), tab(), tab()],
        out_specs=out_specs,
        out_shape=out_shape,
        compiler_params=_cparams(("parallel",)),
        name="rope",
    )(proj, proj, proj, proj, proj, c128, s128, c64, s64)


def _bdot(a, b, dims):
    return lax.dot_general(a.astype(BF16), b.astype(BF16), (dims, ((), ())), preferred_element_type=F32)


def _mm(a, b):
    return _bdot(a, b, ((1,), (0,)))


def _mm_nt(a, b):
    return _bdot(a, b, ((1,), (1,)))


def _mm_tn(a, b):
    return _bdot(a, b, ((0,), (0,)))


def _split_bf16(x, pieces):
    out = []
    for _ in range(pieces - 1):
        hi = x.astype(BF16)
        out.append(hi)
        x = x - hi.astype(F32)
    out.append(x.astype(BF16))
    return out


def _iota2(shape, dim):
    return lax.broadcasted_iota(jnp.int32, shape, dim)


def _log2(n):
    assert n & (n - 1) == 0
    return n.bit_length() - 1


def _tile_rows(x, n):
    return jnp.concatenate([x] * n, axis=0) if n > 1 else x


def _head_sum(x):
    r = _iota2((MXU_DIM, MXU_DIM), 0) >> _log2(HEAD_N)
    c = _iota2((MXU_DIM, MXU_DIM), 1) >> _log2(HEAD_N)
    ones_bd = jnp.where(r == c, 1.0, 0.0).astype(BF16)
    xb = x.astype(BF16)
    outs = [jnp.dot(xb[:, q * MXU_DIM:(q + 1) * MXU_DIM], ones_bd, preferred_element_type=F32)
            for q in range(RWKV_WIDTH // MXU_DIM)]
    return jnp.concatenate(outs, axis=1)


def _rwkv_kernel(shg_ref, prev_ref, *refs, L, t_valid, has_state):
    s0_ref = refs[0] if has_state else None
    (mu_ref, w0_ref, lora_ref, a0_ref, kk_ref, ka_ref, rk_ref, gnw_ref, gnb_ref,
     y_ref, sout_ref, st_ref, last_ref) = refs[1:] if has_state else refs
    _rwkv_body(shg_ref, prev_ref, s0_ref, mu_ref, w0_ref, lora_ref, a0_ref, kk_ref, ka_ref,
               rk_ref, gnw_ref, gnb_ref, y_ref, sout_ref, st_ref, last_ref, L=L, t_valid=t_valid)


def _rwkv_body(shg_ref, prev_ref, s0_ref, mu_ref, w0_ref, lora_ref, a0_ref, kk_ref, ka_ref,
               rk_ref, gnw_ref, gnb_ref, y_ref, sout_ref, st_ref, last_ref, *, L, t_valid):
    c = pl.program_id(1)
    C = RWKV_WIDTH
    nseq = shg_ref.shape[0]

    @pl.when(c == 0)
    def _():
        for ln in range(nseq):
            if s0_ref is None:
                st_ref[ln] = jnp.zeros((HEAD_N, C), F32)
            else:
                st_ref[ln] = jnp.concatenate([s0_ref[ln, h] for h in range(RWKV_HEADS)], axis=1)
        last_ref[...] = prev_ref[...]

    row = _iota2((L, 1), 0)
    seqs = range(nseq)

    r, k, v, g, cwa = [], [], [], [], []
    for ln in seqs:
        x = shg_ref[ln, :, :SHIFT_WIDTH]
        g.append(shg_ref[ln, :, SHIFT_WIDTH:SHIFT_WIDTH + C])
        prev = jnp.where(row == 0, last_ref[ln], pltpu.roll(x, 1, 0))
        last_ref[ln] = x[L - 1:L, :]
        xs = x + (prev - x) * mu_ref[...]
        r.append(xs[:, 0:C])
        k.append(xs[:, C:2 * C])
        v.append(xs[:, 2 * C:3 * C])
        code = xs[:, 3 * C:3 * C + 2 * LORA]
        cwa.append(jnp.where(_iota2(code.shape, 1) < LORA, jnp.tanh(code), code))
    za = [jnp.dot(cwa[ln].astype(BF16), lora_ref[...], preferred_element_type=F32) for ln in seqs]
    a, ld, kk, kmod = [], [], [], []
    for ln in seqs:
        z = w0_ref[...] + za[ln][:, :C]
        a.append(1.0 / (1.0 + jnp.exp(-(a0_ref[...] + za[ln][:, C:]))))
        ld.append((-math.exp(-0.5)) / (1.0 + jnp.exp(-z)))
        kk.append(k[ln] * kk_ref[...])
        kmod.append(k[ln] * (1.0 + (a[ln] - 1.0) * ka_ref[...]))
    sums = [_head_sum(jnp.concatenate([kk[ln] * kk[ln], r[ln] * kmod[ln] * rk_ref[...]], axis=0)) for ln in seqs]
    bonus = []
    for ln in seqs:
        kk[ln] = kk[ln] / jnp.maximum(jnp.sqrt(sums[ln][:L]), 1e-12)
        bonus.append(sums[ln][L:] * v[ln])
        if t_valid < L:
            valid = row < t_valid
            ld[ln] = jnp.where(valid, ld[ln], 0.0)
            kk[ln] = jnp.where(valid, kk[ln], 0.0)
            kmod[ln] = jnp.where(valid, kmod[ln], 0.0)
            v[ln] = jnp.where(valid, v[ln], 0.0)

    tri = jnp.where(_iota2((L, L), 0) >= _iota2((L, L), 1), 1.0, 0.0).astype(BF16)
    cum = [sum(jnp.dot(tri, p, preferred_element_type=F32) for p in _split_bf16(ld[ln], 2)) for ln in seqs]
    al, rb, bt, kt, bend, kend, p_all = [], [], [], [], [], [], []
    for ln in seqs:
        cum_end = cum[ln][L - 1:L, :]
        p_inv = jnp.exp(-cum[ln])
        p_end = jnp.exp(cum_end - cum[ln])
        kka = kk[ln] * a[ln]
        al.append(-kk[ln] * jnp.exp(cum[ln] - ld[ln]))
        rb.append(r[ln] * jnp.exp(cum[ln]))
        bt.append(kka * p_inv)
        kt.append(kmod[ln] * p_inv)
        bend.append(kka * p_end)
        kend.append(kmod[ln] * p_end)
        p_all.append(jnp.exp(cum_end))

    HG = min(RWKV_HEADS, MXU_DIM // L)
    GW = HG * HEAD_N
    SW = HG * L
    lg_l, lg_n = _log2(L), _log2(HEAD_N)
    bd_sc = (_iota2((SW, GW), 0) >> lg_l) == (_iota2((SW, GW), 1) >> lg_n)
    bd_ss = (_iota2((SW, SW), 0) >> lg_l) == (_iota2((SW, SW), 1) >> lg_l)
    t_idx = _iota2((L, SW), 0)
    s_idx = _iota2((L, SW), 1) & (L - 1)
    strict = t_idx > s_idx
    incl = t_idx >= s_idx
    ident = jnp.where(t_idx == s_idx, 1.0, 0.0).astype(F32)
    bd_cc = (_iota2((MXU_DIM, MXU_DIM), 0) >> lg_n) == (_iota2((MXU_DIM, MXU_DIM), 1) >> lg_n)
    col_h = _iota2((HEAD_N, MXU_DIM), 1) >> lg_n
    hp = MXU_DIM // HEAD_N
    nsub = GW // MXU_DIM

    def bd_time(m):
        return jnp.where(bd_ss, _tile_rows(m, HG), 0.0)

    def bd_chan(m):
        return jnp.where(bd_sc, _tile_rows(m, HG), 0.0)

    units = [(ln, gi) for ln in seqs for gi in range(RWKV_HEADS // HG)]

    def gsl(gi):
        return slice(gi * GW, (gi + 1) * GW)

    def qsl(gi, q):
        return slice(gi * GW + q * MXU_DIM, gi * GW + (q + 1) * MXU_DIM)

    a_ab, a_rb, a_ak, a_rk, v_bd = {}, {}, {}, {}, {}
    for u in units:
        ln, gi = u
        lhs = jnp.concatenate([al[ln][:, gsl(gi)], rb[ln][:, gsl(gi)]], axis=0)
        gb = _mm_nt(lhs, bd_chan(bt[ln][:, gsl(gi)]))
        gk = _mm_nt(lhs, bd_chan(kt[ln][:, gsl(gi)]))
        a_ab[u] = jnp.where(strict, gb[:L], 0.0)
        a_rb[u] = jnp.where(incl, gb[L:], 0.0)
        a_ak[u] = jnp.where(strict, gk[:L], 0.0)
        a_rk[u] = jnp.where(incl, gk[L:], 0.0)
        v_bd[u] = bd_chan(v[ln][:, gsl(gi)])
    tinv = {u: ident + a_ab[u] for u in units}
    pw = dict(a_ab)
    for _ in range(lg_l - 1):
        for u in units:
            pw[u] = _mm(pw[u], bd_time(pw[u]))
        for u in units:
            tinv[u] = tinv[u] + _mm(tinv[u], bd_time(pw[u]))
    s_term, av = {}, {}
    for u in units:
        ln, gi = u
        parts = []
        for q in range(nsub):
            qs = qsl(gi, q)
            sv_bd = jnp.where(bd_cc, _tile_rows(st_ref[ln, :, qs], hp), 0.0)
            parts.append(_mm_nt(jnp.concatenate([al[ln][:, qs], rb[ln][:, qs]], axis=0), sv_bd))
        s_term[u] = jnp.concatenate(parts, axis=1) if nsub > 1 else parts[0]
        av[u] = _mm(a_ak[u], v_bd[u])
    uu = {u: _mm(tinv[u], bd_chan(s_term[u][:L] + av[u])) for u in units}
    y_unit = {u: s_term[u][L:] + _mm(a_rb[u], bd_chan(uu[u])) + _mm(a_rk[u], v_bd[u]) for u in units}
    for u in units:
        ln, gi = u
        for q in range(nsub):
            qs = qsl(gi, q)
            ql = slice(q * MXU_DIM, (q + 1) * MXU_DIM)
            vu = jnp.concatenate([v[ln][:, qs], uu[u][:, ql]], axis=0)
            kb = jnp.concatenate([kend[ln][:, qs], bend[ln][:, qs]], axis=0)
            d = _mm_tn(vu, kb)
            upd = st_ref[ln, :, qs] * p_all[ln][:, qs]
            for h in range(hp):
                upd = upd + jnp.where(col_h == h, d[h * HEAD_N:(h + 1) * HEAD_N, :], 0.0)
            st_ref[ln, :, qs] = upd
    ngrp = RWKV_HEADS // HG
    y = [jnp.concatenate([y_unit[(ln, gi)] for gi in range(ngrp)], axis=1) if ngrp > 1 else y_unit[(ln, 0)]
         for ln in seqs]
    mean = [_head_sum(y[ln]) * (1.0 / HEAD_N) for ln in seqs]
    yc = [y[ln] - mean[ln] for ln in seqs]
    var = [_head_sum(yc[ln] * yc[ln]) * (1.0 / HEAD_N) for ln in seqs]
    for ln in seqs:
        yn = yc[ln] * lax.rsqrt(var[ln] + GN_EPS) * gnw_ref[...] + gnb_ref[...]
        y_ref[ln] = ((yn + bonus[ln]) * _silu(g[ln])).astype(y_ref.dtype)

    @pl.when(c == pl.num_programs(1) - 1)
    def _():
        for ln in range(nseq):
            for h in range(RWKV_HEADS):
                sout_ref[ln, h] = st_ref[ln, :, h * HEAD_N:(h + 1) * HEAD_N]


RWKV_SEQS_PER_STEP = 4


def _rwkv(proj, prev_shift, s0, params, nbatch, tper, L, t_valid):
    mu, w0, lora, a0, k_k, k_a, r_k, gn_w, gn_b = params
    C = RWKV_WIDTH
    nchunk = tper // L
    nbl = _pick_tile(nbatch, RWKV_SEQS_PER_STEP)
    vec = lambda n: pl.BlockSpec((1, n), lambda b, c: (0, 0))
    state_spec = pl.BlockSpec((nbl, RWKV_HEADS, HEAD_N, HEAD_N), lambda b, c: (b, 0, 0, 0))
    y, s_fin = pl.pallas_call(
        functools.partial(_rwkv_kernel, L=L, t_valid=t_valid, has_state=s0 is not None),
        grid=(nbatch // nbl, nchunk),
        in_specs=[pl.BlockSpec((nbl, L, RWKV_COLS), lambda b, c: (b, c, 0)),
                  pl.BlockSpec((nbl, 1, SHIFT_WIDTH), lambda b, c: (b, 0, 0))]
                 + ([state_spec] if s0 is not None else [])
                 + [vec(SHIFT_WIDTH), vec(C),
                    pl.BlockSpec((2 * LORA, 2 * C), lambda b, c: (0, 0)),
                    vec(C), vec(C), vec(C), vec(C), vec(C), vec(C)],
        out_specs=[pl.BlockSpec((nbl, L, C), lambda b, c: (b, c, 0)), state_spec],
        out_shape=[jax.ShapeDtypeStruct((nbatch, tper, C), BF16),
                   jax.ShapeDtypeStruct((nbatch, RWKV_HEADS, HEAD_N, HEAD_N), F32)],
        scratch_shapes=[pltpu.VMEM((nbl, HEAD_N, C), F32), pltpu.VMEM((nbl, 1, SHIFT_WIDTH), F32)],
        compiler_params=_cparams(("parallel", "arbitrary")),
        name="rwkv",
    )(proj.reshape(nbatch, tper, PROJ_WIDTH), prev_shift.reshape(nbatch, 1, SHIFT_WIDTH),
      *([s0] if s0 is not None else []),
      mu.reshape(1, -1), w0.reshape(1, -1), lora, a0.reshape(1, -1), k_k.reshape(1, -1),
      k_a.reshape(1, -1), r_k.reshape(1, -1), gn_w.reshape(1, -1), gn_b.reshape(1, -1))
    return y.reshape(nbatch * tper, C), s_fin


def _ordered_int_to_float(s):
    neg_inf_pos = jnp.int32(-2139095041)
    s = jnp.maximum(s, neg_inf_pos)
    return pltpu.bitcast(jnp.where(s < 0, s ^ jnp.int32(0x7FFFFFFF), s), F32)


def _tree_sum(xs):
    while len(xs) > 1:
        xs = [xs[i] + xs[i + 1] for i in range(0, len(xs) - 1, 2)] + ([xs[-1]] if len(xs) % 2 else [])
    return xs[0]


def _count_where(sc_ref, nblk, blk_shape, key_axis, thr, strict):
    def hit(j):
        sc = sc_ref[j]
        return jnp.where(sc > thr if strict else sc >= thr, 1.0, 0.0)

    if isinstance(nblk, int):
        acc = _tree_sum([hit(j) for j in range(nblk)])
    else:
        acc = lax.fori_loop(0, nblk, lambda j, acc: acc + hit(j), jnp.zeros(blk_shape, F32))
    return jnp.sum(acc, axis=key_axis, keepdims=True)


def _kth_largest(sc_ref, nblk, blk_shape, key_axis, topk):
    int_min = jnp.int32(-2147483648)
    qshape = tuple(1 if ax == key_axis else n for ax, n in enumerate(blk_shape))
    bits = 2 if isinstance(nblk, int) else 1

    def step(t, ans):
        shift = jnp.int32(32 - bits) - t * bits
        digit = jnp.zeros(qshape, jnp.int32)
        for k in range(1, 2 ** bits):
            cand = _ordered_int_to_float((ans | lax.shift_left(jnp.int32(k), shift)) ^ int_min)
            cnt = _count_where(sc_ref, nblk, blk_shape, key_axis, cand, False)
            digit = digit + jnp.where(cnt >= topk, 1, 0)
        return ans | jnp.left_shift(digit, shift)

    ans = lax.fori_loop(0, 32 // bits, step, jnp.zeros(qshape, jnp.int32))
    return _ordered_int_to_float(ans ^ int_min)


def _select_topk(sc_ref, nblk, blk_shape, key_axis, topk, causal_fn, store_fn):
    static = isinstance(nblk, int)
    thr = _kth_largest(sc_ref, nblk, blk_shape, key_axis, topk)
    n_gt = _count_where(sc_ref, nblk, blk_shape, key_axis, thr, True)
    n_ge = _count_where(sc_ref, nblk, blk_shape, key_axis, thr, False)
    tied = jnp.max(n_ge) > topk

    @pl.when(jnp.logical_not(tied))
    def _():
        if static:
            store_fn([(sc_ref[j] >= thr) & causal_fn(j) for j in range(nblk)])
        else:
            def sel_blk(j, carry):
                store_fn(j, (sc_ref[j] >= thr) & causal_fn(j))
                return carry

            lax.fori_loop(0, nblk, sel_blk, 0)

    @pl.when(tied)
    def _():
        need = topk - n_gt
        nk = blk_shape[key_axis]
        if key_axis == 1:
            tri = jnp.where(_iota2((nk, nk), 0) <= _iota2((nk, nk), 1), 1.0, 0.0).astype(BF16)
            prefix = lambda eq: jnp.dot(eq, tri, preferred_element_type=F32)
            last = lambda pref: pref[:, nk - 1:nk]
        else:
            tri = jnp.where(_iota2((nk, nk), 0) >= _iota2((nk, nk), 1), 1.0, 0.0).astype(BF16)
            prefix = lambda eq: jnp.dot(tri, eq, preferred_element_type=F32)
            last = lambda pref: pref[nk - 1:nk, :]

        def sel_blk(j, seen):
            sc = sc_ref[j]
            eq = sc == thr
            pref = prefix(jnp.where(eq, 1.0, 0.0).astype(BF16))
            sel = ((sc > thr) | (eq & (seen + pref <= need))) & causal_fn(j)
            return sel, seen + last(pref)

        seen = jnp.zeros(n_gt.shape, F32)
        if static:
            sels = []
            for j in range(nblk):
                sel, seen = sel_blk(j, seen)
                sels.append(sel)
            store_fn(sels)
        else:
            def body(j, seen):
                sel, seen = sel_blk(j, seen)
                store_fn(j, sel)
                return seen

            lax.fori_loop(0, nblk, body, seen)


def _silu(g):
    return g / (1.0 + jnp.exp(-g))


DSA_ROWS = 256
KEY_BLOCK = LANE
EXTENT_KEYS = 512


def _dsa_prompt_kernel(qihm_ref, kw_ref, ki_ref, q_ref, k_ref, vt_ref, ga_ref, o_ref,
                       sc_ref, bias_ref, *, topk, seq, QR):
    i = pl.program_id(1)
    KB = KEY_BLOCK
    nkb = seq // KB
    nblk = (i + 1) * (QR // KB)
    wt = jnp.transpose(kw_ref[...])[IDX_DIM:IDX_DIM + IDX_HEADS] * (IDX_DIM ** -0.5 * IDX_HEADS ** -0.5)
    qpos = i * QR + _iota2((1, QR), 1)

    def causal(j):
        return (j * KB + _iota2((KB, 1), 0)) <= qpos

    per_iter = QR // KB

    def score_blks(jj, carry):
        js = [jj * per_iter + n for n in range(per_iter)]
        dots = [[jnp.dot(ki_ref[pl.ds(pl.multiple_of(j * KB, KB), KB), :], qihm_ref[0, h],
                         preferred_element_type=F32) for h in range(IDX_HEADS)] for j in js]
        for j, d in zip(js, dots):
            acc = _tree_sum([jnp.maximum(d[h], 0.0) * wt[h:h + 1] for h in range(IDX_HEADS)])
            sc_ref[j] = jnp.where(causal(j), acc, -jnp.inf)
        return carry

    lax.fori_loop(0, i + 1, score_blks, 0)

    def store(j, sel):
        bias_ref[j] = jnp.where(sel, 0.0, NEG)

    _select_topk(sc_ref, nblk, (KB, QR), 0, topk, causal, store)

    eb = EXTENT_KEYS // KB
    ext = (nblk - 1) // eb

    def mask_future(j, carry):
        bias_ref[j] = jnp.full((KB, QR), NEG, F32)
        return carry

    lax.fori_loop(nblk, jnp.minimum((ext + 1) * eb, nkb), mask_future, 0)
    scale = ATT_HEAD_DIM ** -0.5

    def attend(nb):
        keys = nb * KB
        bias = jnp.concatenate([bias_ref[j] for j in range(nb)], axis=0)
        heads = [slice(h * ATT_HEAD_DIM, (h + 1) * ATT_HEAD_DIM) for h in range(ATT_HEADS)]
        logits = [lax.dot_general(k_ref[0:keys, hs], q_ref[:, hs], (((1,), (1,)), ((), ())),
                                  preferred_element_type=F32) for hs in heads]
        probs, denom = [], []
        for s in logits:
            s = s * scale + bias
            p = jnp.exp(s - jnp.max(s, axis=0, keepdims=True))
            denom.append(jnp.sum(p, axis=0, keepdims=True))
            probs.append(p.astype(BF16))
        outs = [jnp.transpose(jnp.dot(vt_ref[0, hs, 0:keys], p, preferred_element_type=F32) / l)
                for hs, p, l in zip(heads, probs, denom)]
        o_ref[...] = (jnp.concatenate(outs, axis=1) * _silu(ga_ref[...])).astype(o_ref.dtype)

    for e in range(-(-nkb // eb)):
        pl.when(ext == e)(functools.partial(attend, min((e + 1) * eb, nkb)))


def _dsa_prompt(proj, qihm, ki_b, q_r, k_b, v_t, nbatch, seq, topk):
    QR = _pick_tile(seq, DSA_ROWS)
    assert QR % KEY_BLOCK == 0 and seq % KEY_BLOCK == 0
    nqb = seq // QR
    nkb = seq // KEY_BLOCK
    rowblk = lambda width, colblk: pl.BlockSpec((QR, width), lambda b, i: (b * nqb + i, colblk))
    seqblk = lambda width: pl.BlockSpec((seq, width), lambda b, i: (b, 0))
    return pl.pallas_call(
        functools.partial(_dsa_prompt_kernel, topk=topk, seq=seq, QR=QR),
        grid=(nbatch, nqb),
        in_specs=[pl.BlockSpec((1, IDX_HEADS, IDX_DIM, QR), lambda b, i: (b, 0, 0, i)),
                  rowblk(LANE, COL_KW // LANE),
                  seqblk(IDX_DIM),
                  rowblk(ATT_WIDTH, 0),
                  seqblk(ATT_WIDTH),
                  pl.BlockSpec((1, ATT_WIDTH, seq), lambda b, i: (b, 0, 0)),
                  rowblk(ATT_WIDTH, COL_GA // ATT_WIDTH)],
        out_specs=rowblk(ATT_WIDTH, 0),
        out_shape=jax.ShapeDtypeStruct((nbatch * seq, ATT_WIDTH), BF16),
        scratch_shapes=[pltpu.VMEM((nkb, KEY_BLOCK, QR), F32),
                        pltpu.VMEM((nkb, KEY_BLOCK, QR), F32)],
        compiler_params=_cparams(("parallel", "arbitrary")),
        name="dsa_prompt",
    )(qihm, proj, ki_b, q_r, k_b, v_t, proj)


PAGES_PER_STEP = 32


INDEX_SEQS_PER_STEP = 2


def _dsa_sample_index_kernel(pt_ref, qihm_ref, kw_ref, kinew_ref, *rest, npages, topk, nseq):
    page_refs = rest[:nseq * npages]
    sel4_ref, selnew_ref, sc_ref = rest[nseq * npages:]
    R = SAMPLE_ROWS
    rows = nseq * R
    w = kw_ref[...][:, IDX_DIM:IDX_DIM + IDX_HEADS] * (IDX_DIM ** -0.5 * IDX_HEADS ** -0.5)
    t_row = _iota2((rows, 1), 0) & (R - 1)
    new_ok = _iota2((1, LANE), 1) <= t_row
    dots, d_new, wbs = [], [], []
    for a in range(nseq):
        qi2 = qihm_ref[a].reshape(IDX_HEADS * R, IDX_DIM).astype(BF16)
        wa = w[a * R:(a + 1) * R]
        wbs.append([jnp.broadcast_to(wa[:, h:h + 1], (R, LANE)) for h in range(IDX_HEADS)])
        dots.append([jnp.dot(qi2, page_refs[a * npages + p][0, 0].astype(BF16), preferred_element_type=F32)
                     for p in range(npages)])
        knew = jnp.concatenate([kinew_ref[a * R:(a + 1) * R, :], jnp.zeros((LANE - R, IDX_DIM), F32)],
                               axis=0).astype(BF16)
        d_new.append(lax.dot_general(qi2, knew, (((1,), (1,)), ((), ())), preferred_element_type=F32))

    def weigh(a, d):
        return _tree_sum([jnp.maximum(d[h * R:(h + 1) * R], 0.0) * wbs[a][h] for h in range(IDX_HEADS)])

    def stack(parts):
        return jnp.concatenate(parts, axis=0) if nseq > 1 else parts[0]

    for p in range(npages):
        sc_ref[p] = stack([weigh(a, dots[a][p]) for a in range(nseq)])
    sc_ref[npages] = jnp.where(new_ok, stack([weigh(a, d_new[a]) for a in range(nseq)]), -jnp.inf)

    def causal(j):
        return new_ok if j == npages else jnp.full((rows, LANE), True)

    lg_h = _log2(ATT_HEADS)
    expand = jnp.where(_iota2((LANE, LANE * ATT_HEADS), 0) == (_iota2((LANE, LANE * ATT_HEADS), 1) >> lg_h),
                       1.0, 0.0).astype(BF16)

    def store(sels):
        picked = [jnp.where(sel, 1.0, 0.0) for sel in sels]
        for a in range(nseq):
            pages = jnp.concatenate([m[a * R:(a + 1) * R] for m in picked[:npages]], axis=0)
            sel4_ref[a] = jnp.dot(pages.astype(BF16), expand, preferred_element_type=F32)
            selnew_ref[a] = picked[npages][a * R:(a + 1) * R]

    _select_topk(sc_ref, npages + 1, (rows, LANE), 1, topk, causal, store)


def _dsa_sample_index(proj_s, qihm_s, ki_new, cache_kidx, page_table, topk, layer):
    nbatch, npages = page_table.shape
    R = SAMPLE_ROWS
    nseq = _pick_tile(nbatch, INDEX_SEQS_PER_STEP)

    cache_kidx = jnp.swapaxes(cache_kidx, 2, 3)

    def page_spec(a, p):
        return pl.BlockSpec((1, 1, IDX_DIM, PAGE_SIZE), lambda b, pt, a=a, p=p: (layer, pt[b * nseq + a, p], 0, 0))

    grid_spec = pltpu.PrefetchScalarGridSpec(
        num_scalar_prefetch=1,
        grid=(nbatch // nseq,),
        in_specs=[pl.BlockSpec((nseq, IDX_HEADS, R, IDX_DIM), lambda b, pt: (b, 0, 0, 0)),
                  pl.BlockSpec((nseq * R, LANE), lambda b, pt: (b, COL_KW // LANE)),
                  pl.BlockSpec((nseq * R, IDX_DIM), lambda b, pt: (b, 0))]
                 + [page_spec(a, p) for a in range(nseq) for p in range(npages)],
        out_specs=[pl.BlockSpec((nseq, npages * R, LANE * ATT_HEADS), lambda b, pt: (b, 0, 0)),
                   pl.BlockSpec((nseq, R, LANE), lambda b, pt: (b, 0, 0))],
        scratch_shapes=[pltpu.VMEM((npages + 1, nseq * R, LANE), F32)],
    )
    return pl.pallas_call(
        functools.partial(_dsa_sample_index_kernel, npages=npages, topk=topk, nseq=nseq),
        grid_spec=grid_spec,
        out_shape=[jax.ShapeDtypeStruct((nbatch, npages * R, LANE * ATT_HEADS), F32),
                   jax.ShapeDtypeStruct((nbatch, R, LANE), F32)],
        compiler_params=_cparams(("parallel",)),
        name="dsa_sample_index",
    )(page_table, qihm_s, proj_s, ki_new, *([cache_kidx] * (nseq * npages)))


def _dsa_sample_attn_kernel(pt_ref, sel4_ref, selnew_ref, q_ref, knew_ref, vnew_ref, ga_ref, *rest):
    k_refs = rest[:PAGES_PER_STEP]
    v_refs = rest[PAGES_PER_STEP:2 * PAGES_PER_STEP]
    o_ref, m_ref, l_ref, acc_ref = rest[2 * PAGES_PER_STEP:]
    g = pl.program_id(1)
    R = SAMPLE_ROWS
    H = ATT_HEADS
    HR = H * R
    PW = PAGE_SIZE * H
    scale = ATT_HEAD_DIM ** -0.5

    @pl.when(g == 0)
    def _():
        m_ref[...] = jnp.full(m_ref.shape, NEG, F32)
        l_ref[...] = jnp.zeros(l_ref.shape, F32)
        acc_ref[...] = jnp.zeros(acc_ref.shape, F32)

    q = q_ref[...]
    qhq = jnp.concatenate([q[:, h * ATT_HEAD_DIM:(h + 1) * ATT_HEAD_DIM] for h in range(H)], axis=0).astype(BF16)

    def update(s, sel, pv_fn):
        m = m_ref[:, 0:1]
        l = l_ref[:, 0:1]
        m_new = jnp.maximum(m, jnp.max(jnp.where(sel, s, NEG), axis=1, keepdims=True))
        alpha = jnp.exp(m - m_new)
        p = jnp.where(sel, jnp.exp(s - m_new), 0.0)
        l_new = alpha * l + jnp.sum(p, axis=1, keepdims=True)
        acc_ref[...] = alpha * acc_ref[...] + pv_fn(p)
        m_ref[...] = jnp.broadcast_to(m_new, (HR, LANE))
        l_ref[...] = jnp.broadcast_to(l_new, (HR, LANE))

    npg = PAGES_PER_STEP
    s = jnp.concatenate(
        [lax.dot_general(qhq, k_refs[mth][0, 0].astype(BF16), (((1,), (1,)), ((), ())),
                         preferred_element_type=F32) for mth in range(npg)], axis=1) * scale
    picked = jnp.concatenate([_tile_rows(sel4_ref[0, mth * R:(mth + 1) * R, :], H) for mth in range(npg)],
                             axis=1) > 0.5
    row_h = _iota2((HR, npg * PW), 0) >> _log2(R)
    col_h = _iota2((HR, npg * PW), 1) & (H - 1)

    def pv_pages(p):
        out = jnp.zeros((HR, ATT_HEAD_DIM), F32)
        for mth in range(npg):
            out = out + jnp.dot(p[:, mth * PW:(mth + 1) * PW].astype(BF16), v_refs[mth][0, 0].astype(BF16),
                                preferred_element_type=F32)
        return out

    update(s, picked & (row_h == col_h), pv_pages)

    @pl.when(g == pl.num_programs(1) - 1)
    def _():
        pad = jnp.zeros((LANE - R, ATT_WIDTH), F32)
        knew = jnp.concatenate([knew_ref[...], pad], axis=0).astype(BF16)
        vnew = jnp.concatenate([vnew_ref[...], pad], axis=0).astype(BF16)
        s_new = jnp.concatenate(
            [lax.dot_general(q[:, h * ATT_HEAD_DIM:(h + 1) * ATT_HEAD_DIM].astype(BF16),
                             knew[:, h * ATT_HEAD_DIM:(h + 1) * ATT_HEAD_DIM],
                             (((1,), (1,)), ((), ())), preferred_element_type=F32) for h in range(H)],
            axis=0) * scale

        def pv_new(p):
            return jnp.concatenate(
                [jnp.dot(p[h * R:(h + 1) * R].astype(BF16), vnew[:, h * ATT_HEAD_DIM:(h + 1) * ATT_HEAD_DIM],
                         preferred_element_type=F32) for h in range(H)], axis=0)

        update(s_new, _tile_rows(selnew_ref[0], H) > 0.5, pv_new)
        out = acc_ref[...] / l_ref[:, 0:1]
        out = jnp.concatenate([out[h * R:(h + 1) * R] for h in range(H)], axis=1)
        o_ref[...] = (out * _silu(ga_ref[...])).astype(o_ref.dtype)


def _dsa_sample_attn(proj_s, sel4, selnew, q_r, k_new, v_new, cache_k, cache_v, page_table, layer):
    nbatch, npages = page_table.shape
    ngrp = npages // PAGES_PER_STEP
    R = SAMPLE_ROWS
    PW = PAGE_SIZE * ATT_HEADS
    n_layers, n_pool = cache_k.shape[:2]
    cache_k = cache_k.reshape(n_layers, n_pool, PW, ATT_HEAD_DIM)
    cache_v = cache_v.reshape(n_layers, n_pool, PW, ATT_HEAD_DIM)

    def page_spec(mth):
        return pl.BlockSpec((1, 1, PW, ATT_HEAD_DIM),
                            lambda b, g, pt, mth=mth: (layer, pt[b, g * PAGES_PER_STEP + mth], 0, 0))

    rows = lambda width, colblk: pl.BlockSpec((R, width), lambda b, g, pt: (b, colblk))
    grid_spec = pltpu.PrefetchScalarGridSpec(
        num_scalar_prefetch=1,
        grid=(nbatch, ngrp),
        in_specs=[pl.BlockSpec((1, PAGES_PER_STEP * R, PW), lambda b, g, pt: (b, g, 0)),
                  pl.BlockSpec((1, R, LANE), lambda b, g, pt: (b, 0, 0)),
                  rows(ATT_WIDTH, 0), rows(ATT_WIDTH, 0), rows(ATT_WIDTH, 0),
                  rows(ATT_WIDTH, COL_GA // ATT_WIDTH)]
                 + [page_spec(mth) for mth in range(PAGES_PER_STEP)] * 2,
        out_specs=rows(ATT_WIDTH, 0),
        scratch_shapes=[pltpu.VMEM((ATT_HEADS * R, LANE), F32), pltpu.VMEM((ATT_HEADS * R, LANE), F32),
                        pltpu.VMEM((ATT_HEADS * R, ATT_HEAD_DIM), F32)],
    )
    return pl.pallas_call(
        _dsa_sample_attn_kernel,
        grid_spec=grid_spec,
        out_shape=jax.ShapeDtypeStruct((nbatch * R, ATT_WIDTH), BF16),
        compiler_params=_cparams(("parallel", "arbitrary")),
        name="dsa_sample_attn",
    )(page_table, sel4, selnew, q_r, k_new, v_new, proj_s,
      *([cache_k] * PAGES_PER_STEP), *([cache_v] * PAGES_PER_STEP))


def _mem_attn_kernel(q_ref, mk_ref, mv_ref, gm_ref, o_ref, *, cache_layout):
    scale = MEM_HEAD_DIM ** -0.5
    H = MEM_HEADS
    if cache_layout:
        R = q_ref.shape[0]
        q = q_ref[...]
        qhq = jnp.concatenate([q[:, h * MEM_HEAD_DIM:(h + 1) * MEM_HEAD_DIM] for h in range(H)], axis=0)
        s = lax.dot_general(qhq.astype(BF16), mk_ref[0, 0].astype(BF16), (((1,), (1,)), ((), ())),
                            preferred_element_type=F32) * scale
        own = (_iota2(s.shape, 0) >> _log2(R)) == (_iota2(s.shape, 1) & (H - 1))
        s = jnp.where(own, s, NEG)
        p = jnp.exp(s - jnp.max(s, axis=1, keepdims=True))
        p = p / jnp.sum(p, axis=1, keepdims=True)
        out = jnp.dot(p.astype(BF16), mv_ref[0, 0].astype(BF16), preferred_element_type=F32)
        out = jnp.concatenate([out[h * R:(h + 1) * R] for h in range(H)], axis=1)
    else:
        outs = []
        for h in range(H):
            hs = slice(h * MEM_HEAD_DIM, (h + 1) * MEM_HEAD_DIM)
            s = lax.dot_general(q_ref[:, hs].astype(BF16), mk_ref[:, hs].astype(BF16),
                                (((1,), (1,)), ((), ())), preferred_element_type=F32) * scale
            p = jnp.exp(s - jnp.max(s, axis=1, keepdims=True))
            p = p / jnp.sum(p, axis=1, keepdims=True)
            outs.append(jnp.dot(p.astype(BF16), mv_ref[:, hs].astype(BF16), preferred_element_type=F32))
        out = jnp.concatenate(outs, axis=1)
    o_ref[...] = (out * _silu(gm_ref[...])).astype(o_ref.dtype)


def _mem_attn(proj, mk, mv, nbatch, tper, tq, layer=None):
    nq = tper // tq
    if layer is None:
        mem_tokens = mk.shape[0] // nbatch
        mem_specs = [pl.BlockSpec((mem_tokens, MEM_WIDTH), lambda b, i: (b, 0)),
                     pl.BlockSpec((mem_tokens, MEM_WIDTH), lambda b, i: (b, 1))]
    else:
        n_layers, n_seq, mem_tokens = mk.shape[:3]
        mk = mk.reshape(n_layers, n_seq, mem_tokens * MEM_HEADS, MEM_HEAD_DIM)
        mv = mv.reshape(n_layers, n_seq, mem_tokens * MEM_HEADS, MEM_HEAD_DIM)
        mem_specs = [pl.BlockSpec((1, 1, mem_tokens * MEM_HEADS, MEM_HEAD_DIM), lambda b, i: (layer, b, 0, 0))] * 2
    return pl.pallas_call(
        functools.partial(_mem_attn_kernel, cache_layout=layer is not None),
        grid=(nbatch, nq),
        in_specs=[pl.BlockSpec((tq, MEM_WIDTH), lambda b, i: (b * nq + i, COL_QM // MEM_WIDTH))]
                 + mem_specs
                 + [pl.BlockSpec((tq, MEM_WIDTH), lambda b, i: (b * nq + i, COL_GM // MEM_WIDTH))],
        out_specs=pl.BlockSpec((tq, MEM_WIDTH), lambda b, i: (b * nq + i, 0)),
        out_shape=jax.ShapeDtypeStruct((nbatch * tper, MEM_WIDTH), BF16),
        compiler_params=_cparams(("parallel", "arbitrary")),
        name="mem_attn",
    )(proj, mk, mv, proj)


def _out_proj_kernel(yr_ref, ya_ref, ym_ref, x_ref, w_ref, fnw_ref, o_ref):
    C = RWKV_WIDTH
    y = jnp.dot(yr_ref[...], w_ref[0:C, :], preferred_element_type=F32)
    y = y + jnp.dot(ya_ref[...], w_ref[C:C + ATT_WIDTH, :], preferred_element_type=F32)
    y = y + jnp.dot(ym_ref[...], w_ref[C + ATT_WIDTH:, :], preferred_element_type=F32)
    h = x_ref[...] + y
    ms = jnp.mean(h * h, axis=-1, keepdims=True)
    o_ref[...] = h * lax.rsqrt(ms + RMS_EPS) * fnw_ref[...]


def _out_proj(yr, ya, ym, x, w_out_b, fnw, tm):
    m = x.shape[0]
    rows = lambda width: pl.BlockSpec((tm, width), lambda i: (i, 0))
    return pl.pallas_call(
        _out_proj_kernel,
        grid=(m // tm,),
        in_specs=[rows(RWKV_WIDTH), rows(ATT_WIDTH), rows(MEM_WIDTH), rows(D_MODEL),
                  pl.BlockSpec((D_MODEL, D_MODEL), lambda i: (0, 0)),
                  pl.BlockSpec((1, D_MODEL), lambda i: (0, 0))],
        out_specs=rows(D_MODEL),
        out_shape=jax.ShapeDtypeStruct((m, D_MODEL), F32),
        compiler_params=_cparams(("parallel",)),
        name="out_proj",
    )(yr, ya, ym, x, w_out_b, fnw.reshape(1, D_MODEL))


RELAYOUT_UNITS = 4


def _relayout_kernel(from_tail_ref, src_ref, *refs):
    main_refs, tail_refs, o_ref = refs[:RELAYOUT_UNITS], refs[RELAYOUT_UNITS:2 * RELAYOUT_UNITS], refs[-1]
    for n in range(RELAYOUT_UNITS):
        u = pl.program_id(0) * RELAYOUT_UNITS + n
        cols = slice(n * LANE, (n + 1) * LANE)

        @pl.when(from_tail_ref[u] == 0)
        def _(n=n, cols=cols):
            o_ref[:, cols] = jnp.transpose(main_refs[n][0]).astype(BF16)

        @pl.when(from_tail_ref[u] == 1)
        def _(n=n, cols=cols):
            o_ref[:, cols] = jnp.transpose(tail_refs[n][...]).astype(BF16)


def _relayout_w_in(w_in, layer):
    C = RWKV_WIDTH
    d = w_in.shape[1]
    off_q = SHIFT_WIDTH + C
    off_k, off_v, off_ga = off_q + ATT_WIDTH, off_q + 2 * ATT_WIDTH, off_q + 3 * ATT_WIDTH
    off_qi = off_q + 4 * ATT_WIDTH
    off_wi = off_qi + IDX_HEADS * IDX_DIM
    off_ki = off_wi + IDX_HEADS
    off_qm = off_ki + IDX_DIM
    off_gm = off_qm + MEM_WIDTH
    w_t = jnp.swapaxes(w_in, 1, 2)
    w = w_t[layer]
    tail = jnp.concatenate([w[off_ki:off_qm], w[off_wi:off_ki],
                            jnp.zeros((COL_Q - COL_KW - IDX_DIM - IDX_HEADS, d), w.dtype),
                            w[off_qm:off_gm + MEM_WIDTH]], axis=0)
    from_tail, src = [], []
    for dst, width, from_main_at in ((COL_SH, COL_KW, 0), (COL_KW, COL_Q - COL_KW, None), (COL_Q, ATT_WIDTH, off_q),
                                     (COL_QI, IDX_HEADS * IDX_DIM, off_qi), (COL_K, ATT_WIDTH, off_k),
                                     (COL_V, ATT_WIDTH, off_v), (COL_GA, ATT_WIDTH, off_ga),
                                     (COL_QM, 2 * MEM_WIDTH, None)):
        assert dst == len(src) * LANE and width % LANE == 0
        for t in range(width // LANE):
            from_tail.append(int(from_main_at is None))
            src.append((from_main_at // LANE + t) if from_main_at is not None
                       else ((dst - COL_KW) // LANE + t if dst == COL_KW else (COL_Q - COL_KW) // LANE + t))
    n_units = PROJ_WIDTH // LANE
    assert len(src) == n_units and tail.shape[0] == (COL_Q - COL_KW) + 2 * MEM_WIDTH
    ru = RELAYOUT_UNITS
    assert n_units % ru == 0

    def main_spec(n):
        return pl.BlockSpec((1, LANE, d), lambda s, ft, sr: (layer, jnp.where(ft[s * ru + n] == 0, sr[s * ru + n], 0), 0))

    def tail_spec(n):
        return pl.BlockSpec((LANE, d), lambda s, ft, sr: (jnp.where(ft[s * ru + n] == 1, sr[s * ru + n], 0), 0))

    grid_spec = pltpu.PrefetchScalarGridSpec(
        num_scalar_prefetch=2,
        grid=(n_units // ru,),
        in_specs=[main_spec(n) for n in range(ru)] + [tail_spec(n) for n in range(ru)],
        out_specs=pl.BlockSpec((d, ru * LANE), lambda s, ft, sr: (0, s)),
    )
    return pl.pallas_call(
        _relayout_kernel,
        grid_spec=grid_spec,
        out_shape=jax.ShapeDtypeStruct((d, PROJ_WIDTH), BF16),
        compiler_params=_cparams(("arbitrary",)),
        name="relayout_w_in",
    )(jnp.asarray(from_tail, jnp.int32), jnp.asarray(src, jnp.int32), *([w_t] * ru), *([tail] * ru))


def _pick_tile(m, target):
    t = min(m, target)
    while m % t:
        t //= 2
    return t


def kernel(x_prompt, x_sample, state_wkv, state_shift, cache_k, cache_v, cache_kidx, cache_mem_k, cache_mem_v, page_table, mem_prompt, norm_w, w_in, shift_mu, w0, w2, a0, a2, k_k, k_a, r_k, gn_w, gn_b, mem_norm_w, w_mem_kv, w_out, final_norm_w):
    B, S, D = x_prompt.shape
    DB, T, _ = x_sample.shape
    R = SAMPLE_ROWS
    assert D == D_MODEL and T <= R and S % Q_BLOCK == 0
    n_pages = page_table.shape[1]
    past = n_pages * PAGE_SIZE
    n_pool = cache_k.shape[1]
    mem_tokens = mem_prompt.shape[1]
    topk_p = min(TOPK_MAX, S // 4)
    topk_s = min(TOPK_MAX, (past + T) // 4)
    C = RWKV_WIDTH
    layer = 0

    w_in_b = _relayout_w_in(w_in, layer)
    w_mem_b = w_mem_kv[layer].astype(BF16)
    w_out_b = w_out[layer].astype(BF16)
    zeros_l = jnp.zeros((LORA, C), F32)
    lora = jnp.concatenate([jnp.concatenate([w2[layer], zeros_l], axis=1),
                            jnp.concatenate([zeros_l, a2[layer]], axis=1)], axis=0).astype(BF16)
    rw = (shift_mu[layer], w0[layer], lora, a0[layer], k_k[layer], k_a[layer], r_k[layer].reshape(C),
          gn_w[layer], gn_b[layer])

    xp = x_prompt.reshape(B * S, D)
    xs = jnp.pad(x_sample, ((0, 0), (0, R - T), (0, 0))).reshape(DB * R, D)
    proj_p = _norm_matmul(xp, norm_w[layer], w_in_b, _pick_tile(B * S, 1024), 512)
    proj_s = _norm_matmul(xs, norm_w[layer], w_in_b, _pick_tile(DB * R, 256), PROJ_WIDTH // 4)
    memkv = _norm_matmul(mem_prompt.reshape(B * mem_tokens, D), mem_norm_w[layer], w_mem_b,
                         _pick_tile(B * mem_tokens, 512), 2 * MEM_WIDTH)

    tab_p = _rope_tables(jnp.arange(S))
    tab_s = _rope_tables(past + jnp.arange(R))
    q_p, qihm_p, kf_p, kb_p, vf_p, vb_p, kif_p, kib_p = _rope(
        proj_p, tab_p, B, S, _pick_tile(S, 256), BF16, BF16)
    q_s, qihm_s, kf_s, _, vf_s, _, kif_s, _ = _rope(
        proj_s, tab_s, DB, R, DB * R, F32, F32)

    L_p = _pick_tile(S, 64)
    yr_p, sfin_p = _rwkv(proj_p, jnp.zeros((B, SHIFT_WIDTH), F32), None, rw, B, S, L_p, L_p)
    yr_s, sfin_s = _rwkv(proj_s, state_shift[layer], state_wkv[layer], rw, DB, R, R, T)

    ya_p = _dsa_prompt(proj_p, qihm_p, kib_p, q_p, kb_p, vb_p, B, S, topk_p)
    sel4_s, selnew_s = _dsa_sample_index(proj_s, qihm_s, kif_s, cache_kidx, page_table, topk_s, layer)
    ya_s = _dsa_sample_attn(proj_s, sel4_s, selnew_s, q_s, kf_s, vf_s, cache_k, cache_v, page_table, layer)

    ym_p = _mem_attn(proj_p, memkv, memkv, B, S, _pick_tile(S, 1024))
    ym_s = _mem_attn(proj_s, cache_mem_k, cache_mem_v, DB, R, R, layer)

    out_p = _out_proj(yr_p, ya_p, ym_p, xp, w_out_b, final_norm_w, _pick_tile(B * S, 512))
    out_s = _out_proj(yr_s, ya_s, ym_s, xs, w_out_b, final_norm_w, _pick_tile(DB * R, 256))

    def samp(a, *tail):
        return a.reshape(DB, R, *tail)[:, :T]

    y_prompt = out_p.reshape(B, S, D)
    y_sample = samp(out_s, D)
    new_wkv_prompt = sfin_p[None]
    new_shift_prompt = proj_p.reshape(B, S, PROJ_WIDTH)[:, S - 1, :SHIFT_WIDTH][None]
    new_k_prompt = kf_p[None]
    new_v_prompt = vf_p[None]
    new_kidx_prompt = jnp.swapaxes(kif_p, 1, 2)[None]
    new_mem_k_prompt = memkv[:, :MEM_WIDTH].reshape(B, mem_tokens, MEM_HEADS, MEM_HEAD_DIM)[None]
    new_mem_v_prompt = memkv[:, MEM_WIDTH:].reshape(B, mem_tokens, MEM_HEADS, MEM_HEAD_DIM)[None]
    new_wkv_sample = sfin_s[None]
    new_shift_sample = proj_s.reshape(DB, R, PROJ_WIDTH)[:, T - 1, :SHIFT_WIDTH][None]
    new_k_sample = samp(kf_s, ATT_HEADS, ATT_HEAD_DIM)[None]
    new_v_sample = samp(vf_s, ATT_HEADS, ATT_HEAD_DIM)[None]
    new_kidx_sample = samp(kif_s, IDX_DIM)[None]
    return (y_prompt, y_sample, new_wkv_prompt, new_shift_prompt, new_k_prompt, new_v_prompt,
            new_kidx_prompt, new_mem_k_prompt, new_mem_v_prompt, new_wkv_sample,
            new_shift_sample, new_k_sample, new_v_sample, new_kidx_sample)
```

```python
import functools
import math

import jax
import jax.numpy as jnp
from jax import lax
from jax.experimental import pallas as pl
from jax.experimental.pallas import tpu as pltpu

F32 = jnp.float32
BF16 = jnp.bfloat16

D_MODEL = 2048
PAGE_SIZE = 128
HEAD_N = 64
RWKV_WIDTH = D_MODEL // 2
RWKV_HEADS = RWKV_WIDTH // HEAD_N
LORA = 64
GN_EPS = 64e-5
SHIFT_WIDTH = 3 * RWKV_WIDTH + 2 * LORA
ATT_HEADS = 4
ATT_WIDTH = D_MODEL // 4
ATT_HEAD_DIM = ATT_WIDTH // ATT_HEADS
IDX_HEADS = 16
IDX_DIM = 64
TOPK_MAX = 256
Q_BLOCK = 128
ROPE_THETA = 10000.0
MEM_HEADS = 4
MEM_WIDTH = D_MODEL // 4
MEM_HEAD_DIM = MEM_WIDTH // MEM_HEADS
RMS_EPS = 1e-6

LANE = 128
SUBLANE = 8
MXU_DIM = 256
VMEM_LIMIT = 56 * 1024 * 1024
NEG = -1e30
SAMPLE_ROWS = SUBLANE

COL_SH = 0
COL_GR = SHIFT_WIDTH
COL_KW = COL_GR + RWKV_WIDTH
COL_Q = COL_KW + 3 * LANE
COL_QI = COL_Q + ATT_WIDTH
COL_K = COL_QI + IDX_HEADS * IDX_DIM
COL_V = COL_K + ATT_WIDTH
COL_GA = COL_V + ATT_WIDTH
COL_QM = COL_GA + ATT_WIDTH
COL_GM = COL_QM + MEM_WIDTH
PROJ_WIDTH = COL_GM + MEM_WIDTH
RWKV_COLS = COL_KW


def _cparams(sem):
    return pltpu.CompilerParams(dimension_semantics=sem, vmem_limit_bytes=VMEM_LIMIT)


def _norm_matmul_kernel(x_ref, nw_ref, w_ref, o_ref, xn_ref):
    @pl.when(pl.program_id(1) == 0)
    def _():
        x = x_ref[...]
        ms = jnp.mean(x * x, axis=-1, keepdims=True)
        xn_ref[...] = (x * lax.rsqrt(ms + RMS_EPS) * nw_ref[...]).astype(BF16)

    o_ref[...] = jnp.dot(xn_ref[...], w_ref[...], preferred_element_type=F32)


def _norm_matmul(x, nw, w, tm, tn):
    m, k = x.shape
    n = w.shape[1]
    assert m % tm == 0 and n % tn == 0
    return pl.pallas_call(
        _norm_matmul_kernel,
        grid=(m // tm, n // tn),
        in_specs=[pl.BlockSpec((tm, k), lambda i, j: (i, 0)),
                  pl.BlockSpec((1, k), lambda i, j: (0, 0)),
                  pl.BlockSpec((k, tn), lambda i, j: (0, j))],
        out_specs=pl.BlockSpec((tm, tn), lambda i, j: (i, j)),
        out_shape=jax.ShapeDtypeStruct((m, n), F32),
        scratch_shapes=[pltpu.VMEM((tm, k), BF16)],
        compiler_params=_cparams(("parallel", "arbitrary")),
        name="norm_matmul",
    )(x, nw.reshape(1, k), w)


def _rope_tables(pos):
    def table(d):
        half = d // 2
        inv_freq = ROPE_THETA ** (-jnp.arange(half, dtype=F32) * (2.0 / d))
        ang = pos.astype(F32)[:, None] * inv_freq[None, :]
        cos, sin = jnp.cos(ang), jnp.sin(ang)
        reps = LANE // d
        return (jnp.tile(jnp.concatenate([cos, cos], axis=1), (1, reps)),
                jnp.tile(jnp.concatenate([-sin, sin], axis=1), (1, reps)))
    c128, s128 = table(ATT_HEAD_DIM)
    c64, s64 = table(IDX_DIM)
    return c128, s128, c64, s64


def _rot_half128(x):
    return pltpu.roll(x, LANE // 2, 1)


def _rot_half64(x):
    lane = lax.broadcasted_iota(jnp.int32, x.shape, 1)
    first = (lane & (IDX_DIM - 1)) < IDX_DIM // 2
    return jnp.where(first, pltpu.roll(x, LANE - IDX_DIM // 2, 1), pltpu.roll(x, IDX_DIM // 2, 1))


def _rope_kernel(q_ref, qi_ref, k_ref, v_ref, kw_ref, c128_ref, s128_ref, c64_ref, s64_ref,
                 qr_ref, qihm_ref, kf_ref, kb_ref, vf_ref, vb_ref, kif_ref, kib_ref, *, nb, tper):
    c128, s128 = c128_ref[...], s128_ref[...]
    c64, s64 = c64_ref[...], s64_ref[...]
    for h in range(ATT_HEADS):
        sl = slice(h * LANE, (h + 1) * LANE)
        x = q_ref[:, sl]
        qr_ref[:, sl] = (x * c128 + _rot_half128(x) * s128).astype(qr_ref.dtype)
        x = k_ref[:, sl]
        kr = x * c128 + _rot_half128(x) * s128
        kb_ref[:, sl] = kr.astype(BF16)
        if nb == 1:
            kf_ref[0, :, h, :] = kr
            vf_ref[0, :, h, :] = v_ref[:, sl]
        else:
            kf_ref[:, sl] = kr
    v = v_ref[...]
    if nb == 1:
        vb_ref[0] = jnp.transpose(v).astype(BF16)
    else:
        vf_ref[...] = v
        vb_ref[...] = v.astype(BF16)
    for hp in range(IDX_HEADS // 2):
        x = qi_ref[:, hp * LANE:(hp + 1) * LANE]
        xr = x * c64 + _rot_half64(x) * s64
        if nb == 1:
            xt = jnp.transpose(xr).astype(qihm_ref.dtype)
            for sub in range(2):
                qihm_ref[0, 2 * hp + sub] = xt[sub * IDX_DIM:(sub + 1) * IDX_DIM]
        else:
            for sub in range(2):
                piece = xr[:, sub * IDX_DIM:(sub + 1) * IDX_DIM]
                qihm_ref[:, 2 * hp + sub] = piece.reshape(nb, tper, IDX_DIM).astype(qihm_ref.dtype)
    x = kw_ref[...]
    kir_slab = x * c64 + _rot_half64(x) * s64
    kir = kir_slab[:, :IDX_DIM]
    if nb == 1:
        kif_ref[0] = jnp.transpose(kir_slab)[:IDX_DIM]
    else:
        kif_ref[...] = kir
    kib_ref[...] = kir.astype(BF16)


def _rope(proj, tables, nbatch, tper_total, ts, qi_dtype, q_dtype):
    m = proj.shape[0]
    c128, s128, c64, s64 = tables
    if ts >= tper_total:
        nb, tper = ts // tper_total, tper_total
    else:
        nb, tper = 1, ts
    blocks_per_seq = max(tper_total // ts, 1)
    grid = (m // ts,)

    def col(width, start):
        return pl.BlockSpec((ts, width), lambda i: (i, start // width))

    def tab():
        if nb == 1:
            return pl.BlockSpec((ts, LANE), lambda i: (i % blocks_per_seq, 0))
        return pl.BlockSpec((ts, LANE), lambda i: (0, 0))

    def rows(width):
        return pl.BlockSpec((ts, width), lambda i: (i, 0))

    if nb == 1:
        qihm_shape = jax.ShapeDtypeStruct((nbatch, IDX_HEADS, IDX_DIM, tper_total), qi_dtype)
        qihm_spec = pl.BlockSpec((1, IDX_HEADS, IDX_DIM, tper),
                                 lambda i: (i // blocks_per_seq, 0, 0, i % blocks_per_seq))
        vb_shape = jax.ShapeDtypeStruct((nbatch, ATT_WIDTH, tper_total), BF16)
        vb_spec = pl.BlockSpec((1, ATT_WIDTH, ts), lambda i: (i // blocks_per_seq, 0, i % blocks_per_seq))
        kvf_shape = jax.ShapeDtypeStruct((nbatch, tper_total, ATT_HEADS, ATT_HEAD_DIM), F32)
        kvf_spec = pl.BlockSpec((1, ts, ATT_HEADS, ATT_HEAD_DIM),
                                lambda i: (i // blocks_per_seq, i % blocks_per_seq, 0, 0))
        kif_shape = jax.ShapeDtypeStruct((nbatch, IDX_DIM, tper_total), F32)
        kif_spec = pl.BlockSpec((1, IDX_DIM, ts), lambda i: (i // blocks_per_seq, 0, i % blocks_per_seq))
    else:
        kvf_shape, kvf_spec = jax.ShapeDtypeStruct((m, ATT_WIDTH), F32), rows(ATT_WIDTH)
        kif_shape, kif_spec = jax.ShapeDtypeStruct((m, IDX_DIM), F32), rows(IDX_DIM)
        qihm_shape = jax.ShapeDtypeStruct((nbatch, IDX_HEADS, tper_total, IDX_DIM), qi_dtype)
        qihm_spec = pl.BlockSpec((nb, IDX_HEADS, tper, IDX_DIM), lambda i: (i, 0, 0, 0))
        vb_shape = jax.ShapeDtypeStruct((m, ATT_WIDTH), BF16)
        vb_spec = rows(ATT_WIDTH)
    if nb > 1:
        c128, s128, c64, s64 = (jnp.tile(t, (nb, 1)) for t in (c128, s128, c64, s64))
    out_shape = [
        jax.ShapeDtypeStruct((m, ATT_WIDTH), q_dtype),
        qihm_shape,
        kvf_shape,
        jax.ShapeDtypeStruct((m, ATT_WIDTH), BF16),
        kvf_shape,
        vb_shape,
        kif_shape,
        jax.ShapeDtypeStruct((m, IDX_DIM), BF16),
    ]
    out_specs = [rows(ATT_WIDTH), qihm_spec, kvf_spec, rows(ATT_WIDTH), kvf_spec,
                 vb_spec, kif_spec, rows(IDX_DIM)]
    return pl.pallas_call(
        functools.partial(_rope_kernel, nb=nb, tper=tper),
        grid=grid,
        in_specs=[col(ATT_WIDTH, COL_Q), col(IDX_HEADS * IDX_DIM, COL_QI), col(ATT_WIDTH, COL_K),
                  col(ATT_WIDTH, COL_V), col(LANE, COL_KW), tab(), tab(), tab(), tab()],
        out_specs=out_specs,
        out_shape=out_shape,
        compiler_params=_cparams(("parallel",)),
        name="rope",
    )(proj, proj, proj, proj, proj, c128, s128, c64, s64)


def _bdot(a, b, dims):
    return lax.dot_general(a.astype(BF16), b.astype(BF16), (dims, ((), ())), preferred_element_type=F32)


def _mm(a, b):
    return _bdot(a, b, ((1,), (0,)))


def _mm_nt(a, b):
    return _bdot(a, b, ((1,), (1,)))


def _mm_tn(a, b):
    return _bdot(a, b, ((0,), (0,)))


def _split_bf16(x, pieces):
    out = []
    for _ in range(pieces - 1):
        hi = x.astype(BF16)
        out.append(hi)
        x = x - hi.astype(F32)
    out.append(x.astype(BF16))
    return out


def _iota2(shape, dim):
    return lax.broadcasted_iota(jnp.int32, shape, dim)


def _log2(n):
    assert n & (n - 1) == 0
    return n.bit_length() - 1


def _tile_rows(x, n):
    return jnp.concatenate([x] * n, axis=0) if n > 1 else x


def _head_sum(x):
    r = _iota2((MXU_DIM, MXU_DIM), 0) >> _log2(HEAD_N)
    c = _iota2((MXU_DIM, MXU_DIM), 1) >> _log2(HEAD_N)
    ones_bd = jnp.where(r == c, 1.0, 0.0).astype(BF16)
    xb = x.astype(BF16)
    outs = [jnp.dot(xb[:, q * MXU_DIM:(q + 1) * MXU_DIM], ones_bd, preferred_element_type=F32)
            for q in range(RWKV_WIDTH // MXU_DIM)]
    return jnp.concatenate(outs, axis=1)


def _rwkv_kernel(shg_ref, prev_ref, *refs, L, t_valid, has_state):
    s0_ref = refs[0] if has_state else None
    (mu_ref, w0_ref, lora_ref, a0_ref, kk_ref, ka_ref, rk_ref, gnw_ref, gnb_ref,
     y_ref, sout_ref, st_ref, last_ref) = refs[1:] if has_state else refs
    _rwkv_body(shg_ref, prev_ref, s0_ref, mu_ref, w0_ref, lora_ref, a0_ref, kk_ref, ka_ref,
               rk_ref, gnw_ref, gnb_ref, y_ref, sout_ref, st_ref, last_ref, L=L, t_valid=t_valid)


def _rwkv_body(shg_ref, prev_ref, s0_ref, mu_ref, w0_ref, lora_ref, a0_ref, kk_ref, ka_ref,
               rk_ref, gnw_ref, gnb_ref, y_ref, sout_ref, st_ref, last_ref, *, L, t_valid):
    c = pl.program_id(1)
    C = RWKV_WIDTH
    nseq = shg_ref.shape[0]

    @pl.when(c == 0)
    def _():
        for ln in range(nseq):
            if s0_ref is None:
                st_ref[ln] = jnp.zeros((HEAD_N, C), F32)
            else:
                st_ref[ln] = jnp.concatenate([s0_ref[ln, h] for h in range(RWKV_HEADS)], axis=1)
        last_ref[...] = prev_ref[...]

    row = _iota2((L, 1), 0)
    seqs = range(nseq)

    r, k, v, g, cwa = [], [], [], [], []
    for ln in seqs:
        x = shg_ref[ln, :, :SHIFT_WIDTH]
        g.append(shg_ref[ln, :, SHIFT_WIDTH:SHIFT_WIDTH + C])
        prev = jnp.where(row == 0, last_ref[ln], pltpu.roll(x, 1, 0))
        last_ref[ln] = x[L - 1:L, :]
        xs = x + (prev - x) * mu_ref[...]
        r.append(xs[:, 0:C])
        k.append(xs[:, C:2 * C])
        v.append(xs[:, 2 * C:3 * C])
        code = xs[:, 3 * C:3 * C + 2 * LORA]
        cwa.append(jnp.where(_iota2(code.shape, 1) < LORA, jnp.tanh(code), code))
    za = [jnp.dot(cwa[ln].astype(BF16), lora_ref[...], preferred_element_type=F32) for ln in seqs]
    a, ld, kk, kmod = [], [], [], []
    for ln in seqs:
        z = w0_ref[...] + za[ln][:, :C]
        a.append(1.0 / (1.0 + jnp.exp(-(a0_ref[...] + za[ln][:, C:]))))
        ld.append((-math.exp(-0.5)) / (1.0 + jnp.exp(-z)))
        kk.append(k[ln] * kk_ref[...])
        kmod.append(k[ln] * (1.0 + (a[ln] - 1.0) * ka_ref[...]))
    sums = [_head_sum(jnp.concatenate([kk[ln] * kk[ln], r[ln] * kmod[ln] * rk_ref[...]], axis=0)) for ln in seqs]
    bonus = []
    for ln in seqs:
        kk[ln] = kk[ln] / jnp.maximum(jnp.sqrt(sums[ln][:L]), 1e-12)
        bonus.append(sums[ln][L:] * v[ln])
        if t_valid < L:
            valid = row < t_valid
            ld[ln] = jnp.where(valid, ld[ln], 0.0)
            kk[ln] = jnp.where(valid, kk[ln], 0.0)
            kmod[ln] = jnp.where(valid, kmod[ln], 0.0)
            v[ln] = jnp.where(valid, v[ln], 0.0)

    tri = jnp.where(_iota2((L, L), 0) >= _iota2((L, L), 1), 1.0, 0.0).astype(BF16)
    cum = [sum(jnp.dot(tri, p, preferred_element_type=F32) for p in _split_bf16(ld[ln], 2)) for ln in seqs]
    al, rb, bt, kt, bend, kend, p_all = [], [], [], [], [], [], []
    for ln in seqs:
        cum_end = cum[ln][L - 1:L, :]
        p_inv = jnp.exp(-cum[ln])
        p_end = jnp.exp(cum_end - cum[ln])
        kka = kk[ln] * a[ln]
        al.append(-kk[ln] * jnp.exp(cum[ln] - ld[ln]))
        rb.append(r[ln] * jnp.exp(cum[ln]))
        bt.append(kka * p_inv)
        kt.append(kmod[ln] * p_inv)
        bend.append(kka * p_end)
        kend.append(kmod[ln] * p_end)
        p_all.append(jnp.exp(cum_end))

    HG = min(RWKV_HEADS, MXU_DIM // L)
    GW = HG * HEAD_N
    SW = HG * L
    lg_l, lg_n = _log2(L), _log2(HEAD_N)
    bd_sc = (_iota2((SW, GW), 0) >> lg_l) == (_iota2((SW, GW), 1) >> lg_n)
    bd_ss = (_iota2((SW, SW), 0) >> lg_l) == (_iota2((SW, SW), 1) >> lg_l)
    t_idx = _iota2((L, SW), 0)
    s_idx = _iota2((L, SW), 1) & (L - 1)
    strict = t_idx > s_idx
    incl = t_idx >= s_idx
    ident = jnp.where(t_idx == s_idx, 1.0, 0.0).astype(F32)
    bd_cc = (_iota2((MXU_DIM, MXU_DIM), 0) >> lg_n) == (_iota2((MXU_DIM, MXU_DIM), 1) >> lg_n)
    col_h = _iota2((HEAD_N, MXU_DIM), 1) >> lg_n
    hp = MXU_DIM // HEAD_N
    nsub = GW // MXU_DIM

    def bd_time(m):
        return jnp.where(bd_ss, _tile_rows(m, HG), 0.0)

    def bd_chan(m):
        return jnp.where(bd_sc, _tile_rows(m, HG), 0.0)

    units = [(ln, gi) for ln in seqs for gi in range(RWKV_HEADS // HG)]

    def gsl(gi):
        return slice(gi * GW, (gi + 1) * GW)

    def qsl(gi, q):
        return slice(gi * GW + q * MXU_DIM, gi * GW + (q + 1) * MXU_DIM)

    a_ab, a_rb, a_ak, a_rk, v_bd = {}, {}, {}, {}, {}
    for u in units:
        ln, gi = u
        lhs = jnp.concatenate([al[ln][:, gsl(gi)], rb[ln][:, gsl(gi)]], axis=0)
        gb = _mm_nt(lhs, bd_chan(bt[ln][:, gsl(gi)]))
        gk = _mm_nt(lhs, bd_chan(kt[ln][:, gsl(gi)]))
        a_ab[u] = jnp.where(strict, gb[:L], 0.0)
        a_rb[u] = jnp.where(incl, gb[L:], 0.0)
        a_ak[u] = jnp.where(strict, gk[:L], 0.0)
        a_rk[u] = jnp.where(incl, gk[L:], 0.0)
        v_bd[u] = bd_chan(v[ln][:, gsl(gi)])
    tinv = {u: ident + a_ab[u] for u in units}
    pw = dict(a_ab)
    for _ in range(lg_l - 1):
        for u in units:
            pw[u] = _mm(pw[u], bd_time(pw[u]))
        for u in units:
            tinv[u] = tinv[u] + _mm(tinv[u], bd_time(pw[u]))
    s_term, av = {}, {}
    for u in units:
        ln, gi = u
        parts = []
        for q in range(nsub):
            qs = qsl(gi, q)
            sv_bd = jnp.where(bd_cc, _tile_rows(st_ref[ln, :, qs], hp), 0.0)
            parts.append(_mm_nt(jnp.concatenate([al[ln][:, qs], rb[ln][:, qs]], axis=0), sv_bd))
        s_term[u] = jnp.concatenate(parts, axis=1) if nsub > 1 else parts[0]
        av[u] = _mm(a_ak[u], v_bd[u])
    uu = {u: _mm(tinv[u], bd_chan(s_term[u][:L] + av[u])) for u in units}
    y_unit = {u: s_term[u][L:] + _mm(a_rb[u], bd_chan(uu[u])) + _mm(a_rk[u], v_bd[u]) for u in units}
    for u in units:
        ln, gi = u
        for q in range(nsub):
            qs = qsl(gi, q)
            ql = slice(q * MXU_DIM, (q + 1) * MXU_DIM)
            vu = jnp.concatenate([v[ln][:, qs], uu[u][:, ql]], axis=0)
            kb = jnp.concatenate([kend[ln][:, qs], bend[ln][:, qs]], axis=0)
            d = _mm_tn(vu, kb)
            upd = st_ref[ln, :, qs] * p_all[ln][:, qs]
            for h in range(hp):
                upd = upd + jnp.where(col_h == h, d[h * HEAD_N:(h + 1) * HEAD_N, :], 0.0)
            st_ref[ln, :, qs] = upd
    ngrp = RWKV_HEADS // HG
    y = [jnp.concatenate([y_unit[(ln, gi)] for gi in range(ngrp)], axis=1) if ngrp > 1 else y_unit[(ln, 0)]
         for ln in seqs]
    mean = [_head_sum(y[ln]) * (1.0 / HEAD_N) for ln in seqs]
    yc = [y[ln] - mean[ln] for ln in seqs]
    var = [_head_sum(yc[ln] * yc[ln]) * (1.0 / HEAD_N) for ln in seqs]
    for ln in seqs:
        yn = yc[ln] * lax.rsqrt(var[ln] + GN_EPS) * gnw_ref[...] + gnb_ref[...]
        y_ref[ln] = ((yn + bonus[ln]) * _silu(g[ln])).astype(y_ref.dtype)

    @pl.when(c == pl.num_programs(1) - 1)
    def _():
        for ln in range(nseq):
            for h in range(RWKV_HEADS):
                sout_ref[ln, h] = st_ref[ln, :, h * HEAD_N:(h + 1) * HEAD_N]


RWKV_SEQS_PER_STEP = 4


def _rwkv(proj, prev_shift, s0, params, nbatch, tper, L, t_valid):
    mu, w0, lora, a0, k_k, k_a, r_k, gn_w, gn_b = params
    C = RWKV_WIDTH
    nchunk = tper // L
    nbl = _pick_tile(nbatch, RWKV_SEQS_PER_STEP)
    vec = lambda n: pl.BlockSpec((1, n), lambda b, c: (0, 0))
    state_spec = pl.BlockSpec((nbl, RWKV_HEADS, HEAD_N, HEAD_N), lambda b, c: (b, 0, 0, 0))
    y, s_fin = pl.pallas_call(
        functools.partial(_rwkv_kernel, L=L, t_valid=t_valid, has_state=s0 is not None),
        grid=(nbatch // nbl, nchunk),
        in_specs=[pl.BlockSpec((nbl, L, RWKV_COLS), lambda b, c: (b, c, 0)),
                  pl.BlockSpec((nbl, 1, SHIFT_WIDTH), lambda b, c: (b, 0, 0))]
                 + ([state_spec] if s0 is not None else [])
                 + [vec(SHIFT_WIDTH), vec(C),
                    pl.BlockSpec((2 * LORA, 2 * C), lambda b, c: (0, 0)),
                    vec(C), vec(C), vec(C), vec(C), vec(C), vec(C)],
        out_specs=[pl.BlockSpec((nbl, L, C), lambda b, c: (b, c, 0)), state_spec],
        out_shape=[jax.ShapeDtypeStruct((nbatch, tper, C), BF16),
                   jax.ShapeDtypeStruct((nbatch, RWKV_HEADS, HEAD_N, HEAD_N), F32)],
        scratch_shapes=[pltpu.VMEM((nbl, HEAD_N, C), F32), pltpu.VMEM((nbl, 1, SHIFT_WIDTH), F32)],
        compiler_params=_cparams(("parallel", "arbitrary")),
        name="rwkv",
    )(proj.reshape(nbatch, tper, PROJ_WIDTH), prev_shift.reshape(nbatch, 1, SHIFT_WIDTH),
      *([s0] if s0 is not None else []),
      mu.reshape(1, -1), w0.reshape(1, -1), lora, a0.reshape(1, -1), k_k.reshape(1, -1),
      k_a.reshape(1, -1), r_k.reshape(1, -1), gn_w.reshape(1, -1), gn_b.reshape(1, -1))
    return y.reshape(nbatch * tper, C), s_fin


def _ordered_int_to_float(s):
    neg_inf_pos = jnp.int32(-2139095041)
    s = jnp.maximum(s, neg_inf_pos)
    return pltpu.bitcast(jnp.where(s < 0, s ^ jnp.int32(0x7FFFFFFF), s), F32)


def _tree_sum(xs):
    while len(xs) > 1:
        xs = [xs[i] + xs[i + 1] for i in range(0, len(xs) - 1, 2)] + ([xs[-1]] if len(xs) % 2 else [])
    return xs[0]


def _count_where(sc_ref, nblk, blk_shape, key_axis, thr, strict):
    def hit(j):
        sc = sc_ref[j]
        return jnp.where(sc > thr if strict else sc >= thr, 1.0, 0.0)

    if isinstance(nblk, int):
        acc = _tree_sum([hit(j) for j in range(nblk)])
    else:
        acc = lax.fori_loop(0, nblk, lambda j, acc: acc + hit(j), jnp.zeros(blk_shape, F32))
    return jnp.sum(acc, axis=key_axis, keepdims=True)


def _kth_largest(sc_ref, nblk, blk_shape, key_axis, topk):
    int_min = jnp.int32(-2147483648)
    qshape = tuple(1 if ax == key_axis else n for ax, n in enumerate(blk_shape))
    bits = 2 if isinstance(nblk, int) else 1

    def step(t, ans):
        shift = jnp.int32(32 - bits) - t * bits
        digit = jnp.zeros(qshape, jnp.int32)
        for k in range(1, 2 ** bits):
            cand = _ordered_int_to_float((ans | lax.shift_left(jnp.int32(k), shift)) ^ int_min)
            cnt = _count_where(sc_ref, nblk, blk_shape, key_axis, cand, False)
            digit = digit + jnp.where(cnt >= topk, 1, 0)
        return ans | jnp.left_shift(digit, shift)

    ans = lax.fori_loop(0, 32 // bits, step, jnp.zeros(qshape, jnp.int32))
    return _ordered_int_to_float(ans ^ int_min)


def _select_topk(sc_ref, nblk, blk_shape, key_axis, topk, causal_fn, store_fn):
    static = isinstance(nblk, int)
    thr = _kth_largest(sc_ref, nblk, blk_shape, key_axis, topk)
    n_gt = _count_where(sc_ref, nblk, blk_shape, key_axis, thr, True)
    n_ge = _count_where(sc_ref, nblk, blk_shape, key_axis, thr, False)
    tied = jnp.max(n_ge) > topk

    @pl.when(jnp.logical_not(tied))
    def _():
        if static:
            store_fn([(sc_ref[j] >= thr) & causal_fn(j) for j in range(nblk)])
        else:
            def sel_blk(j, carry):
                store_fn(j, (sc_ref[j] >= thr) & causal_fn(j))
                return carry

            lax.fori_loop(0, nblk, sel_blk, 0)

    @pl.when(tied)
    def _():
        need = topk - n_gt
        nk = blk_shape[key_axis]
        if key_axis == 1:
            tri = jnp.where(_iota2((nk, nk), 0) <= _iota2((nk, nk), 1), 1.0, 0.0).astype(BF16)
            prefix = lambda eq: jnp.dot(eq, tri, preferred_element_type=F32)
            last = lambda pref: pref[:, nk - 1:nk]
        else:
            tri = jnp.where(_iota2((nk, nk), 0) >= _iota2((nk, nk), 1), 1.0, 0.0).astype(BF16)
            prefix = lambda eq: jnp.dot(tri, eq, preferred_element_type=F32)
            last = lambda pref: pref[nk - 1:nk, :]

        def sel_blk(j, seen):
            sc = sc_ref[j]
            eq = sc == thr
            pref = prefix(jnp.where(eq, 1.0, 0.0).astype(BF16))
            sel = ((sc > thr) | (eq & (seen + pref <= need))) & causal_fn(j)
            return sel, seen + last(pref)

        seen = jnp.zeros(n_gt.shape, F32)
        if static:
            sels = []
            for j in range(nblk):
                sel, seen = sel_blk(j, seen)
                sels.append(sel)
            store_fn(sels)
        else:
            def body(j, seen):
                sel, seen = sel_blk(j, seen)
                store_fn(j, sel)
                return seen

            lax.fori_loop(0, nblk, body, seen)


def _silu(g):
    return g / (1.0 + jnp.exp(-g))


DSA_ROWS = 256
KEY_BLOCK = LANE
EXTENT_KEYS = 256


def _dsa_prompt_kernel(qihm_ref, kw_ref, ki_ref, q_ref, k_ref, vt_ref, ga_ref, o_ref,
                       sc_ref, bias_ref, *, topk, seq, QR):
    i = pl.program_id(1)
    KB = KEY_BLOCK
    nkb = seq // KB
    nblk = (i + 1) * (QR // KB)
    wt = jnp.transpose(kw_ref[...])[IDX_DIM:IDX_DIM + IDX_HEADS] * (IDX_DIM ** -0.5 * IDX_HEADS ** -0.5)
    qpos = i * QR + _iota2((1, QR), 1)

    def causal(j):
        return (j * KB + _iota2((KB, 1), 0)) <= qpos

    per_iter = QR // KB

    def score_blks(jj, carry):
        js = [jj * per_iter + n for n in range(per_iter)]
        dots = [[jnp.dot(ki_ref[pl.ds(pl.multiple_of(j * KB, KB), KB), :], qihm_ref[0, h],
                         preferred_element_type=F32) for h in range(IDX_HEADS)] for j in js]
        for j, d in zip(js, dots):
            acc = _tree_sum([jnp.maximum(d[h], 0.0) * wt[h:h + 1] for h in range(IDX_HEADS)])
            sc_ref[j] = jnp.where(causal(j), acc, -jnp.inf)
        return carry

    lax.fori_loop(0, i + 1, score_blks, 0)

    def store(j, sel):
        bias_ref[j] = jnp.where(sel, 0.0, NEG)

    _select_topk(sc_ref, nblk, (KB, QR), 0, topk, causal, store)

    eb = EXTENT_KEYS // KB
    ext = (nblk - 1) // eb

    def mask_future(j, carry):
        bias_ref[j] = jnp.full((KB, QR), NEG, F32)
        return carry

    lax.fori_loop(nblk, jnp.minimum((ext + 1) * eb, nkb), mask_future, 0)
    scale = ATT_HEAD_DIM ** -0.5

    def attend(nb):
        keys = nb * KB
        bias = jnp.concatenate([bias_ref[j] for j in range(nb)], axis=0)
        heads = [slice(h * ATT_HEAD_DIM, (h + 1) * ATT_HEAD_DIM) for h in range(ATT_HEADS)]
        logits = [lax.dot_general(k_ref[0:keys, hs], q_ref[:, hs], (((1,), (1,)), ((), ())),
                                  preferred_element_type=F32) for hs in heads]
        probs, denom = [], []
        for s in logits:
            s = s * scale + bias
            p = jnp.exp(s - jnp.max(s, axis=0, keepdims=True))
            denom.append(jnp.sum(p, axis=0, keepdims=True))
            probs.append(p.astype(BF16))
        outs = [jnp.transpose(jnp.dot(vt_ref[0, hs, 0:keys], p, preferred_element_type=F32) / l)
                for hs, p, l in zip(heads, probs, denom)]
        o_ref[...] = (jnp.concatenate(outs, axis=1) * _silu(ga_ref[...])).astype(o_ref.dtype)

    for e in range(-(-nkb // eb)):
        pl.when(ext == e)(functools.partial(attend, min((e + 1) * eb, nkb)))


def _dsa_prompt(proj, qihm, ki_b, q_r, k_b, v_t, nbatch, seq, topk):
    QR = _pick_tile(seq, DSA_ROWS)
    assert QR % KEY_BLOCK == 0 and seq % KEY_BLOCK == 0
    nqb = seq // QR
    nkb = seq // KEY_BLOCK
    rowblk = lambda width, colblk: pl.BlockSpec((QR, width), lambda b, i: (b * nqb + i, colblk))
    seqblk = lambda width: pl.BlockSpec((seq, width), lambda b, i: (b, 0))
    return pl.pallas_call(
        functools.partial(_dsa_prompt_kernel, topk=topk, seq=seq, QR=QR),
        grid=(nbatch, nqb),
        in_specs=[pl.BlockSpec((1, IDX_HEADS, IDX_DIM, QR), lambda b, i: (b, 0, 0, i)),
                  rowblk(LANE, COL_KW // LANE),
                  seqblk(IDX_DIM),
                  rowblk(ATT_WIDTH, 0),
                  seqblk(ATT_WIDTH),
                  pl.BlockSpec((1, ATT_WIDTH, seq), lambda b, i: (b, 0, 0)),
                  rowblk(ATT_WIDTH, COL_GA // ATT_WIDTH)],
        out_specs=rowblk(ATT_WIDTH, 0),
        out_shape=jax.ShapeDtypeStruct((nbatch * seq, ATT_WIDTH), BF16),
        scratch_shapes=[pltpu.VMEM((nkb, KEY_BLOCK, QR), F32),
                        pltpu.VMEM((nkb, KEY_BLOCK, QR), F32)],
        compiler_params=_cparams(("parallel", "arbitrary")),
        name="dsa_prompt",
    )(qihm, proj, ki_b, q_r, k_b, v_t, proj)


PAGES_PER_STEP = 32


INDEX_SEQS_PER_STEP = 2


def _dsa_sample_index_kernel(pt_ref, qihm_ref, kw_ref, kinew_ref, *rest, npages, topk, nseq):
    page_refs = rest[:nseq * npages]
    sel4_ref, selnew_ref, sc_ref = rest[nseq * npages:]
    R = SAMPLE_ROWS
    rows = nseq * R
    w = kw_ref[...][:, IDX_DIM:IDX_DIM + IDX_HEADS] * (IDX_DIM ** -0.5 * IDX_HEADS ** -0.5)
    t_row = _iota2((rows, 1), 0) & (R - 1)
    new_ok = _iota2((1, LANE), 1) <= t_row
    dots, d_new, wbs = [], [], []
    for a in range(nseq):
        qi2 = qihm_ref[a].reshape(IDX_HEADS * R, IDX_DIM).astype(BF16)
        wa = w[a * R:(a + 1) * R]
        wbs.append([jnp.broadcast_to(wa[:, h:h + 1], (R, LANE)) for h in range(IDX_HEADS)])
        dots.append([jnp.dot(qi2, page_refs[a * npages + p][0, 0].astype(BF16), preferred_element_type=F32)
                     for p in range(npages)])
        knew = jnp.concatenate([kinew_ref[a * R:(a + 1) * R, :], jnp.zeros((LANE - R, IDX_DIM), F32)],
                               axis=0).astype(BF16)
        d_new.append(lax.dot_general(qi2, knew, (((1,), (1,)), ((), ())), preferred_element_type=F32))

    def weigh(a, d):
        return _tree_sum([jnp.maximum(d[h * R:(h + 1) * R], 0.0) * wbs[a][h] for h in range(IDX_HEADS)])

    def stack(parts):
        return jnp.concatenate(parts, axis=0) if nseq > 1 else parts[0]

    for p in range(npages):
        sc_ref[p] = stack([weigh(a, dots[a][p]) for a in range(nseq)])
    sc_ref[npages] = jnp.where(new_ok, stack([weigh(a, d_new[a]) for a in range(nseq)]), -jnp.inf)

    def causal(j):
        return new_ok if j == npages else jnp.full((rows, LANE), True)

    lg_h = _log2(ATT_HEADS)
    expand = jnp.where(_iota2((LANE, LANE * ATT_HEADS), 0) == (_iota2((LANE, LANE * ATT_HEADS), 1) >> lg_h),
                       1.0, 0.0).astype(BF16)

    def store(sels):
        picked = [jnp.where(sel, 1.0, 0.0) for sel in sels]
        for a in range(nseq):
            pages = jnp.concatenate([m[a * R:(a + 1) * R] for m in picked[:npages]], axis=0)
            sel4_ref[a] = jnp.dot(pages.astype(BF16), expand, preferred_element_type=F32)
            selnew_ref[a] = picked[npages][a * R:(a + 1) * R]

    _select_topk(sc_ref, npages + 1, (rows, LANE), 1, topk, causal, store)


def _dsa_sample_index(proj_s, qihm_s, ki_new, cache_kidx, page_table, topk, layer):
    nbatch, npages = page_table.shape
    R = SAMPLE_ROWS
    nseq = _pick_tile(nbatch, INDEX_SEQS_PER_STEP)

    cache_kidx = jnp.swapaxes(cache_kidx, 2, 3)

    def page_spec(a, p):
        return pl.BlockSpec((1, 1, IDX_DIM, PAGE_SIZE), lambda b, pt, a=a, p=p: (layer, pt[b * nseq + a, p], 0, 0))

    grid_spec = pltpu.PrefetchScalarGridSpec(
        num_scalar_prefetch=1,
        grid=(nbatch // nseq,),
        in_specs=[pl.BlockSpec((nseq, IDX_HEADS, R, IDX_DIM), lambda b, pt: (b, 0, 0, 0)),
                  pl.BlockSpec((nseq * R, LANE), lambda b, pt: (b, COL_KW // LANE)),
                  pl.BlockSpec((nseq * R, IDX_DIM), lambda b, pt: (b, 0))]
                 + [page_spec(a, p) for a in range(nseq) for p in range(npages)],
        out_specs=[pl.BlockSpec((nseq, npages * R, LANE * ATT_HEADS), lambda b, pt: (b, 0, 0)),
                   pl.BlockSpec((nseq, R, LANE), lambda b, pt: (b, 0, 0))],
        scratch_shapes=[pltpu.VMEM((npages + 1, nseq * R, LANE), F32)],
    )
    return pl.pallas_call(
        functools.partial(_dsa_sample_index_kernel, npages=npages, topk=topk, nseq=nseq),
        grid_spec=grid_spec,
        out_shape=[jax.ShapeDtypeStruct((nbatch, npages * R, LANE * ATT_HEADS), F32),
                   jax.ShapeDtypeStruct((nbatch, R, LANE), F32)],
        compiler_params=_cparams(("parallel",)),
        name="dsa_sample_index",
    )(page_table, qihm_s, proj_s, ki_new, *([cache_kidx] * (nseq * npages)))


def _dsa_sample_attn_kernel(pt_ref, sel4_ref, selnew_ref, q_ref, knew_ref, vnew_ref, ga_ref, *rest):
    k_refs = rest[:PAGES_PER_STEP]
    v_refs = rest[PAGES_PER_STEP:2 * PAGES_PER_STEP]
    o_ref, m_ref, l_ref, acc_ref = rest[2 * PAGES_PER_STEP:]
    g = pl.program_id(1)
    R = SAMPLE_ROWS
    H = ATT_HEADS
    HR = H * R
    PW = PAGE_SIZE * H
    scale = ATT_HEAD_DIM ** -0.5

    @pl.when(g == 0)
    def _():
        m_ref[...] = jnp.full(m_ref.shape, NEG, F32)
        l_ref[...] = jnp.zeros(l_ref.shape, F32)
        acc_ref[...] = jnp.zeros(acc_ref.shape, F32)

    q = q_ref[...]
    qhq = jnp.concatenate([q[:, h * ATT_HEAD_DIM:(h + 1) * ATT_HEAD_DIM] for h in range(H)], axis=0).astype(BF16)

    def update(s, sel, pv_fn):
        m = m_ref[:, 0:1]
        l = l_ref[:, 0:1]
        m_new = jnp.maximum(m, jnp.max(jnp.where(sel, s, NEG), axis=1, keepdims=True))
        alpha = jnp.exp(m - m_new)
        p = jnp.where(sel, jnp.exp(s - m_new), 0.0)
        l_new = alpha * l + jnp.sum(p, axis=1, keepdims=True)
        acc_ref[...] = alpha * acc_ref[...] + pv_fn(p)
        m_ref[...] = jnp.broadcast_to(m_new, (HR, LANE))
        l_ref[...] = jnp.broadcast_to(l_new, (HR, LANE))

    npg = PAGES_PER_STEP
    s = jnp.concatenate(
        [lax.dot_general(qhq, k_refs[mth][0, 0].astype(BF16), (((1,), (1,)), ((), ())),
                         preferred_element_type=F32) for mth in range(npg)], axis=1) * scale
    picked = jnp.concatenate([_tile_rows(sel4_ref[0, mth * R:(mth + 1) * R, :], H) for mth in range(npg)],
                             axis=1) > 0.5
    row_h = _iota2((HR, npg * PW), 0) >> _log2(R)
    col_h = _iota2((HR, npg * PW), 1) & (H - 1)

    def pv_pages(p):
        out = jnp.zeros((HR, ATT_HEAD_DIM), F32)
        for mth in range(npg):
            out = out + jnp.dot(p[:, mth * PW:(mth + 1) * PW].astype(BF16), v_refs[mth][0, 0].astype(BF16),
                                preferred_element_type=F32)
        return out

    update(s, picked & (row_h == col_h), pv_pages)

    @pl.when(g == pl.num_programs(1) - 1)
    def _():
        pad = jnp.zeros((LANE - R, ATT_WIDTH), F32)
        knew = jnp.concatenate([knew_ref[...], pad], axis=0).astype(BF16)
        vnew = jnp.concatenate([vnew_ref[...], pad], axis=0).astype(BF16)
        s_new = jnp.concatenate(
            [lax.dot_general(q[:, h * ATT_HEAD_DIM:(h + 1) * ATT_HEAD_DIM].astype(BF16),
                             knew[:, h * ATT_HEAD_DIM:(h + 1) * ATT_HEAD_DIM],
                             (((1,), (1,)), ((), ())), preferred_element_type=F32) for h in range(H)],
            axis=0) * scale

        def pv_new(p):
            return jnp.concatenate(
                [jnp.dot(p[h * R:(h + 1) * R].astype(BF16), vnew[:, h * ATT_HEAD_DIM:(h + 1) * ATT_HEAD_DIM],
                         preferred_element_type=F32) for h in range(H)], axis=0)

        update(s_new, _tile_rows(selnew_ref[0], H) > 0.5, pv_new)
        out = acc_ref[...] / l_ref[:, 0:1]
        out = jnp.concatenate([out[h * R:(h + 1) * R] for h in range(H)], axis=1)
        o_ref[...] = (out * _silu(ga_ref[...])).astype(o_ref.dtype)


def _dsa_sample_attn(proj_s, sel4, selnew, q_r, k_new, v_new, cache_k, cache_v, page_table, layer):
    nbatch, npages = page_table.shape
    ngrp = npages // PAGES_PER_STEP
    R = SAMPLE_ROWS
    PW = PAGE_SIZE * ATT_HEADS
    n_layers, n_pool = cache_k.shape[:2]
    cache_k = cache_k.reshape(n_layers, n_pool, PW, ATT_HEAD_DIM)
    cache_v = cache_v.reshape(n_layers, n_pool, PW, ATT_HEAD_DIM)

    def page_spec(mth):
        return pl.BlockSpec((1, 1, PW, ATT_HEAD_DIM),
                            lambda b, g, pt, mth=mth: (layer, pt[b, g * PAGES_PER_STEP + mth], 0, 0))

    rows = lambda width, colblk: pl.BlockSpec((R, width), lambda b, g, pt: (b, colblk))
    grid_spec = pltpu.PrefetchScalarGridSpec(
        num_scalar_prefetch=1,
        grid=(nbatch, ngrp),
        in_specs=[pl.BlockSpec((1, PAGES_PER_STEP * R, PW), lambda b, g, pt: (b, g, 0)),
                  pl.BlockSpec((1, R, LANE), lambda b, g, pt: (b, 0, 0)),
                  rows(ATT_WIDTH, 0), rows(ATT_WIDTH, 0), rows(ATT_WIDTH, 0),
                  rows(ATT_WIDTH, COL_GA // ATT_WIDTH)]
                 + [page_spec(mth) for mth in range(PAGES_PER_STEP)] * 2,
        out_specs=rows(ATT_WIDTH, 0),
        scratch_shapes=[pltpu.VMEM((ATT_HEADS * R, LANE), F32), pltpu.VMEM((ATT_HEADS * R, LANE), F32),
                        pltpu.VMEM((ATT_HEADS * R, ATT_HEAD_DIM), F32)],
    )
    return pl.pallas_call(
        _dsa_sample_attn_kernel,
        grid_spec=grid_spec,
        out_shape=jax.ShapeDtypeStruct((nbatch * R, ATT_WIDTH), BF16),
        compiler_params=_cparams(("parallel", "arbitrary")),
        name="dsa_sample_attn",
    )(page_table, sel4, selnew, q_r, k_new, v_new, proj_s,
      *([cache_k] * PAGES_PER_STEP), *([cache_v] * PAGES_PER_STEP))


def _mem_attn_kernel(q_ref, mk_ref, mv_ref, gm_ref, o_ref, *, cache_layout):
    scale = MEM_HEAD_DIM ** -0.5
    H = MEM_HEADS
    if cache_layout:
        R = q_ref.shape[0]
        q = q_ref[...]
        qhq = jnp.concatenate([q[:, h * MEM_HEAD_DIM:(h + 1) * MEM_HEAD_DIM] for h in range(H)], axis=0)
        s = lax.dot_general(qhq.astype(BF16), mk_ref[0, 0].astype(BF16), (((1,), (1,)), ((), ())),
                            preferred_element_type=F32) * scale
        own = (_iota2(s.shape, 0) >> _log2(R)) == (_iota2(s.shape, 1) & (H - 1))
        s = jnp.where(own, s, NEG)
        p = jnp.exp(s - jnp.max(s, axis=1, keepdims=True))
        p = p / jnp.sum(p, axis=1, keepdims=True)
        out = jnp.dot(p.astype(BF16), mv_ref[0, 0].astype(BF16), preferred_element_type=F32)
        out = jnp.concatenate([out[h * R:(h + 1) * R] for h in range(H)], axis=1)
    else:
        outs = []
        for h in range(H):
            hs = slice(h * MEM_HEAD_DIM, (h + 1) * MEM_HEAD_DIM)
            s = lax.dot_general(q_ref[:, hs].astype(BF16), mk_ref[:, hs].astype(BF16),
                                (((1,), (1,)), ((), ())), preferred_element_type=F32) * scale
            p = jnp.exp(s - jnp.max(s, axis=1, keepdims=True))
            p = p / jnp.sum(p, axis=1, keepdims=True)
            outs.append(jnp.dot(p.astype(BF16), mv_ref[:, hs].astype(BF16), preferred_element_type=F32))
        out = jnp.concatenate(outs, axis=1)
    o_ref[...] = (out * _silu(gm_ref[...])).astype(o_ref.dtype)


def _mem_attn(proj, mk, mv, nbatch, tper, tq, layer=None):
    nq = tper // tq
    if layer is None:
        mem_tokens = mk.shape[0] // nbatch
        mem_specs = [pl.BlockSpec((mem_tokens, MEM_WIDTH), lambda b, i: (b, 0)),
                     pl.BlockSpec((mem_tokens, MEM_WIDTH), lambda b, i: (b, 1))]
    else:
        n_layers, n_seq, mem_tokens = mk.shape[:3]
        mk = mk.reshape(n_layers, n_seq, mem_tokens * MEM_HEADS, MEM_HEAD_DIM)
        mv = mv.reshape(n_layers, n_seq, mem_tokens * MEM_HEADS, MEM_HEAD_DIM)
        mem_specs = [pl.BlockSpec((1, 1, mem_tokens * MEM_HEADS, MEM_HEAD_DIM), lambda b, i: (layer, b, 0, 0))] * 2
    return pl.pallas_call(
        functools.partial(_mem_attn_kernel, cache_layout=layer is not None),
        grid=(nbatch, nq),
        in_specs=[pl.BlockSpec((tq, MEM_WIDTH), lambda b, i: (b * nq + i, COL_QM // MEM_WIDTH))]
                 + mem_specs
                 + [pl.BlockSpec((tq, MEM_WIDTH), lambda b, i: (b * nq + i, COL_GM // MEM_WIDTH))],
        out_specs=pl.BlockSpec((tq, MEM_WIDTH), lambda b, i: (b * nq + i, 0)),
        out_shape=jax.ShapeDtypeStruct((nbatch * tper, MEM_WIDTH), BF16),
        compiler_params=_cparams(("parallel", "arbitrary")),
        name="mem_attn",
    )(proj, mk, mv, proj)


def _out_proj_kernel(yr_ref, ya_ref, ym_ref, x_ref, w_ref, fnw_ref, o_ref):
    C = RWKV_WIDTH
    y = jnp.dot(yr_ref[...], w_ref[0:C, :], preferred_element_type=F32)
    y = y + jnp.dot(ya_ref[...], w_ref[C:C + ATT_WIDTH, :], preferred_element_type=F32)
    y = y + jnp.dot(ym_ref[...], w_ref[C + ATT_WIDTH:, :], preferred_element_type=F32)
    h = x_ref[...] + y
    ms = jnp.mean(h * h, axis=-1, keepdims=True)
    o_ref[...] = h * lax.rsqrt(ms + RMS_EPS) * fnw_ref[...]


def _out_proj(yr, ya, ym, x, w_out_b, fnw, tm):
    m = x.shape[0]
    rows = lambda width: pl.BlockSpec((tm, width), lambda i: (i, 0))
    return pl.pallas_call(
        _out_proj_kernel,
        grid=(m // tm,),
        in_specs=[rows(RWKV_WIDTH), rows(ATT_WIDTH), rows(MEM_WIDTH), rows(D_MODEL),
                  pl.BlockSpec((D_MODEL, D_MODEL), lambda i: (0, 0)),
                  pl.BlockSpec((1, D_MODEL), lambda i: (0, 0))],
        out_specs=rows(D_MODEL),
        out_shape=jax.ShapeDtypeStruct((m, D_MODEL), F32),
        compiler_params=_cparams(("parallel",)),
        name="out_proj",
    )(yr, ya, ym, x, w_out_b, fnw.reshape(1, D_MODEL))


RELAYOUT_UNITS = 4


def _relayout_kernel(from_tail_ref, src_ref, *refs):
    main_refs, tail_refs, o_ref = refs[:RELAYOUT_UNITS], refs[RELAYOUT_UNITS:2 * RELAYOUT_UNITS], refs[-1]
    for n in range(RELAYOUT_UNITS):
        u = pl.program_id(0) * RELAYOUT_UNITS + n
        cols = slice(n * LANE, (n + 1) * LANE)

        @pl.when(from_tail_ref[u] == 0)
        def _(n=n, cols=cols):
            o_ref[:, cols] = jnp.transpose(main_refs[n][0]).astype(BF16)

        @pl.when(from_tail_ref[u] == 1)
        def _(n=n, cols=cols):
            o_ref[:, cols] = jnp.transpose(tail_refs[n][...]).astype(BF16)


def _relayout_w_in(w_in, layer):
    C = RWKV_WIDTH
    d = w_in.shape[1]
    off_q = SHIFT_WIDTH + C
    off_k, off_v, off_ga = off_q + ATT_WIDTH, off_q + 2 * ATT_WIDTH, off_q + 3 * ATT_WIDTH
    off_qi = off_q + 4 * ATT_WIDTH
    off_wi = off_qi + IDX_HEADS * IDX_DIM
    off_ki = off_wi + IDX_HEADS
    off_qm = off_ki + IDX_DIM
    off_gm = off_qm + MEM_WIDTH
    w_t = jnp.swapaxes(w_in, 1, 2)
    w = w_t[layer]
    tail = jnp.concatenate([w[off_ki:off_qm], w[off_wi:off_ki],
                            jnp.zeros((COL_Q - COL_KW - IDX_DIM - IDX_HEADS, d), w.dtype),
                            w[off_qm:off_gm + MEM_WIDTH]], axis=0)
    from_tail, src = [], []
    for dst, width, from_main_at in ((COL_SH, COL_KW, 0), (COL_KW, COL_Q - COL_KW, None), (COL_Q, ATT_WIDTH, off_q),
                                     (COL_QI, IDX_HEADS * IDX_DIM, off_qi), (COL_K, ATT_WIDTH, off_k),
                                     (COL_V, ATT_WIDTH, off_v), (COL_GA, ATT_WIDTH, off_ga),
                                     (COL_QM, 2 * MEM_WIDTH, None)):
        assert dst == len(src) * LANE and width % LANE == 0
        for t in range(width // LANE):
            from_tail.append(int(from_main_at is None))
            src.append((from_main_at // LANE + t) if from_main_at is not None
                       else ((dst - COL_KW) // LANE + t if dst == COL_KW else (COL_Q - COL_KW) // LANE + t))
    n_units = PROJ_WIDTH // LANE
    assert len(src) == n_units and tail.shape[0] == (COL_Q - COL_KW) + 2 * MEM_WIDTH
    ru = RELAYOUT_UNITS
    assert n_units % ru == 0

    def main_spec(n):
        return pl.BlockSpec((1, LANE, d), lambda s, ft, sr: (layer, jnp.where(ft[s * ru + n] == 0, sr[s * ru + n], 0), 0))

    def tail_spec(n):
        return pl.BlockSpec((LANE, d), lambda s, ft, sr: (jnp.where(ft[s * ru + n] == 1, sr[s * ru + n], 0), 0))

    grid_spec = pltpu.PrefetchScalarGridSpec(
        num_scalar_prefetch=2,
        grid=(n_units // ru,),
        in_specs=[main_spec(n) for n in range(ru)] + [tail_spec(n) for n in range(ru)],
        out_specs=pl.BlockSpec((d, ru * LANE), lambda s, ft, sr: (0, s)),
    )
    return pl.pallas_call(
        _relayout_kernel,
        grid_spec=grid_spec,
        out_shape=jax.ShapeDtypeStruct((d, PROJ_WIDTH), BF16),
        compiler_params=_cparams(("arbitrary",)),
        name="relayout_w_in",
    )(jnp.asarray(from_tail, jnp.int32), jnp.asarray(src, jnp.int32), *([w_t] * ru), *([tail] * ru))


def _pick_tile(m, target):
    t = min(m, target)
    while m % t:
        t //= 2
    return t


def kernel(x_prompt, x_sample, state_wkv, state_shift, cache_k, cache_v, cache_kidx, cache_mem_k, cache_mem_v, page_table, mem_prompt, norm_w, w_in, shift_mu, w0, w2, a0, a2, k_k, k_a, r_k, gn_w, gn_b, mem_norm_w, w_mem_kv, w_out, final_norm_w):
    B, S, D = x_prompt.shape
    DB, T, _ = x_sample.shape
    R = SAMPLE_ROWS
    assert D == D_MODEL and T <= R and S % Q_BLOCK == 0
    n_pages = page_table.shape[1]
    past = n_pages * PAGE_SIZE
    n_pool = cache_k.shape[1]
    mem_tokens = mem_prompt.shape[1]
    topk_p = min(TOPK_MAX, S // 4)
    topk_s = min(TOPK_MAX, (past + T) // 4)
    C = RWKV_WIDTH
    layer = 0

    w_in_b = _relayout_w_in(w_in, layer)
    w_mem_b = w_mem_kv[layer].astype(BF16)
    w_out_b = w_out[layer].astype(BF16)
    zeros_l = jnp.zeros((LORA, C), F32)
    lora = jnp.concatenate([jnp.concatenate([w2[layer], zeros_l], axis=1),
                            jnp.concatenate([zeros_l, a2[layer]], axis=1)], axis=0).astype(BF16)
    rw = (shift_mu[layer], w0[layer], lora, a0[layer], k_k[layer], k_a[layer], r_k[layer].reshape(C),
          gn_w[layer], gn_b[layer])

    xp = x_prompt.reshape(B * S, D)
    xs = jnp.pad(x_sample, ((0, 0), (0, R - T), (0, 0))).reshape(DB * R, D)
    proj_p = _norm_matmul(xp, norm_w[layer], w_in_b, _pick_tile(B * S, 1024), 512)
    proj_s = _norm_matmul(xs, norm_w[layer], w_in_b, _pick_tile(DB * R, 256), PROJ_WIDTH // 4)
    memkv = _norm_matmul(mem_prompt.reshape(B * mem_tokens, D), mem_norm_w[layer], w_mem_b,
                         _pick_tile(B * mem_tokens, 512), 2 * MEM_WIDTH)

    tab_p = _rope_tables(jnp.arange(S))
    tab_s = _rope_tables(past + jnp.arange(R))
    q_p, qihm_p, kf_p, kb_p, vf_p, vb_p, kif_p, kib_p = _rope(
        proj_p, tab_p, B, S, _pick_tile(S, 256), BF16, BF16)
    q_s, qihm_s, kf_s, _, vf_s, _, kif_s, _ = _rope(
        proj_s, tab_s, DB, R, DB * R, F32, F32)

    L_p = _pick_tile(S, 64)
    yr_p, sfin_p = _rwkv(proj_p, jnp.zeros((B, SHIFT_WIDTH), F32), None, rw, B, S, L_p, L_p)
    yr_s, sfin_s = _rwkv(proj_s, state_shift[layer], state_wkv[layer], rw, DB, R, R, T)

    ya_p = _dsa_prompt(proj_p, qihm_p, kib_p, q_p, kb_p, vb_p, B, S, topk_p)
    sel4_s, selnew_s = _dsa_sample_index(proj_s, qihm_s, kif_s, cache_kidx, page_table, topk_s, layer)
    ya_s = _dsa_sample_attn(proj_s, sel4_s, selnew_s, q_s, kf_s, vf_s, cache_k, cache_v, page_table, layer)

    ym_p = _mem_attn(proj_p, memkv, memkv, B, S, _pick_tile(S, 1024))
    ym_s = _mem_attn(proj_s, cache_mem_k, cache_mem_v, DB, R, R, layer)

    out_p = _out_proj(yr_p, ya_p, ym_p, xp, w_out_b, final_norm_w, _pick_tile(B * S, 512))
    out_s = _out_proj(yr_s, ya_s, ym_s, xs, w_out_b, final_norm_w, _pick_tile(DB * R, 256))

    def samp(a, *tail):
        return a.reshape(DB, R, *tail)[:, :T]

    y_prompt = out_p.reshape(B, S, D)
    y_sample = samp(out_s, D)
    new_wkv_prompt = sfin_p[None]
    new_shift_prompt = proj_p.reshape(B, S, PROJ_WIDTH)[:, S - 1, :SHIFT_WIDTH][None]
    new_k_prompt = kf_p[None]
    new_v_prompt = vf_p[None]
    new_kidx_prompt = jnp.swapaxes(kif_p, 1, 2)[None]
    new_mem_k_prompt = memkv[:, :MEM_WIDTH].reshape(B, mem_tokens, MEM_HEADS, MEM_HEAD_DIM)[None]
    new_mem_v_prompt = memkv[:, MEM_WIDTH:].reshape(B, mem_tokens, MEM_HEADS, MEM_HEAD_DIM)[None]
    new_wkv_sample = sfin_s[None]
    new_shift_sample = proj_s.reshape(DB, R, PROJ_WIDTH)[:, T - 1, :SHIFT_WIDTH][None]
    new_k_sample = samp(kf_s, ATT_HEADS, ATT_HEAD_DIM)[None]
    new_v_sample = samp(vf_s, ATT_HEADS, ATT_HEAD_DIM)[None]
    new_kidx_sample = samp(kif_s, IDX_DIM)[None]
    return (y_prompt, y_sample, new_wkv_prompt, new_shift_prompt, new_k_prompt, new_v_prompt,
            new_kidx_prompt, new_mem_k_prompt, new_mem_v_prompt, new_wkv_sample,
            new_shift_sample, new_k_sample, new_v_sample, new_kidx_sample)
```
